```python
import jax, jax.numpy as jnp
from jax import lax
import numpy as np

D_MODEL = 2048
BATCH = 4
SEQ = 2048
DEPTH = 4
DEC_BATCH = 128
DEC_SEQ = 8
PAST_LEN = 16384
PAGE_SIZE = 128

N_MIXERS = 3
N_A = (DEPTH + 2) // 3
N_B = (DEPTH + 1) // 3
N_C = DEPTH // 3
CONV_W = 31
D_CONV = D_MODEL
POOL_WINDOWS = (2, 4, 8, 16)
N_POOL_GROUPS = len(POOL_WINDOWS)
POOL_GROUP = D_MODEL // N_POOL_GROUPS
POOL_HIST = max(POOL_WINDOWS) - 1
HGRN_EXPAND = 128
HGRN_HEADS = D_MODEL // HGRN_EXPAND
HGRN_DK = HGRN_EXPAND
HGRN_DV = D_MODEL // HGRN_HEADS
HGRN_KD = HGRN_HEADS * HGRN_DK
HGRN_CHUNK = 64
D_FF = 5632
FFN_CONV_W = 3
NORM_EPS = 1e-6

kernel_name = "hybrid_conv_pool_hgrn2_step"


def rmsnorm(x, g):
    xf = x.astype(jnp.float32)
    y = xf * lax.rsqrt(jnp.mean(xf * xf, axis=-1, keepdims=True) + NORM_EPS) * g.astype(jnp.float32)
    return y.astype(x.dtype)


def layernorm(x, g, b):
    xf = x.astype(jnp.float32)
    mu = jnp.mean(xf, axis=-1, keepdims=True)
    var = jnp.mean(jnp.square(xf - mu), axis=-1, keepdims=True)
    y = (xf - mu) * lax.rsqrt(var + NORM_EPS) * g.astype(jnp.float32) + b.astype(jnp.float32)
    return y.astype(x.dtype)


def causal_dwconv(x, buf, w, b):
    width, ch = w.shape
    xp = jnp.concatenate([buf.astype(x.dtype), x], axis=1)
    y = lax.conv_general_dilated(xp, w.astype(x.dtype)[:, None, :], window_strides=(1,), padding='VALID',
                                 dimension_numbers=('NWC', 'WIO', 'NWC'), feature_group_count=ch)
    return y + b.astype(x.dtype), xp[:, xp.shape[1] - (width - 1):]


def conformer_conv_mixer(h, buf, w_pw1, w_dw, b_dw, ln_g, ln_b, w_pw2):
    u = h @ w_pw1
    a, gate = jnp.split(u, 2, axis=-1)
    v = a * jax.nn.sigmoid(gate)
    c, new_buf = causal_dwconv(v, buf, w_dw, b_dw)
    c = jax.nn.silu(layernorm(c, ln_g, ln_b))
    return c @ w_pw2, new_buf


def pool_mixer(h, buf, start, w_grp, scale):
    B, T, D = h.shape
    hf = h.astype(jnp.float32)
    hp = jnp.concatenate([buf.astype(jnp.float32), hf], axis=1)
    cs = jnp.concatenate([jnp.zeros((B, 1, D), jnp.float32), jnp.cumsum(hp, axis=1)], axis=1)
    pos = start + jnp.arange(T)
    outs = []
    for g, w in enumerate(POOL_WINDOWS):
        sl = slice(g * POOL_GROUP, (g + 1) * POOL_GROUP)
        hi = cs[:, POOL_HIST + 1:POOL_HIST + 1 + T, sl]
        lo = cs[:, POOL_HIST + 1 - w:POOL_HIST + 1 - w + T, sl]
        cnt = jnp.minimum(w, pos + 1).astype(jnp.float32)[None, :, None]
        outs.append((hi - lo) / cnt)
    pooled = (jnp.concatenate(outs, axis=-1) - hf).astype(h.dtype).reshape(B, T, N_POOL_GROUPS, POOL_GROUP)
    y = jnp.einsum('btgc,gcd->btgd', pooled, w_grp).reshape(B, T, D) * scale
    return y, hp[:, hp.shape[1] - POOL_HIST:].astype(h.dtype)


def hgrn2_mixer(h, s0, lb, w_in, g_norm, w_o):
    B, T, _ = h.shape
    proj = h @ w_in
    q, fz, inp, g = jnp.split(proj, [HGRN_KD, 2 * HGRN_KD, 2 * HGRN_KD + D_MODEL], axis=-1)
    f = lb + (1.0 - lb) * jax.nn.sigmoid(fz.astype(jnp.float32))
    logf = jnp.log(f)
    k = 1.0 - f
    q = jax.nn.silu(q.astype(jnp.float32))
    inp = inp.astype(jnp.float32)
    C = min(HGRN_CHUNK, T)
    n = -(-T // C)
    pad = n * C - T

    def chunks(t):
        t = jnp.pad(t, ((0, 0), (0, pad), (0, 0)))
        return t.reshape(B, n, C, HGRN_HEADS, -1).transpose(1, 0, 3, 2, 4)

    mask = jnp.tril(jnp.ones((C, C), bool))[:, :, None]

    def step(S, xs):
        qc, kc, lfc, ic = xs
        G = jnp.cumsum(lfc, axis=2)
        inter = jnp.einsum('bhtk,bhkv->bhtv', qc * jnp.exp(G), S)
        decay = jnp.exp(jnp.where(mask, G[:, :, :, None, :] - G[:, :, None, :, :], -jnp.inf))
        scores = jnp.einsum('bhtk,bhtsk->bhts', qc, decay * kc[:, :, None, :, :])
        o = inter + jnp.einsum('bhts,bhsv->bhtv', scores, ic)
        G_end = G[:, :, -1, :]
        S = jnp.exp(G_end)[..., None] * S + jnp.einsum('bhsk,bhsv->bhkv', kc * jnp.exp(G_end[:, :, None, :] - G), ic)
        return S, o

    S_fin, o = lax.scan(step, s0.astype(jnp.float32), (chunks(q), chunks(k), chunks(logf), chunks(inp)))
    o = o.transpose(1, 0, 3, 2, 4).reshape(B, n * C, HGRN_HEADS, HGRN_DV)[:, :T]
    o = o * lax.rsqrt(jnp.mean(o * o, axis=-1, keepdims=True) + NORM_EPS)
    o = o.reshape(B, T, D_MODEL) * g_norm.astype(jnp.float32) * jax.nn.silu(g.astype(jnp.float32))
    return o.astype(h.dtype) @ w_o, S_fin.astype(s0.dtype)


def conv_ffn(h, buf, w_up, w_dw, b_dw, w_down):
    u = h @ w_up
    c, new_buf = causal_dwconv(u, buf, w_dw, b_dw)
    a, b = jnp.split(c, 2, axis=-1)
    return (jax.nn.silu(a) * b) @ w_down, new_buf


def trunk(x, st_a, st_b, st_c, st_f, start, norm_mix, norm_ffn, norm_final,
          a_w_pw1, a_w_dw, a_b_dw, a_ln_g, a_ln_b, a_w_pw2, b_w_grp, b_scale,
          c_lb, c_w_in, c_g_norm, c_w_o, f_w_up, f_w_dw, f_b_dw, f_w_down):
    lb_all = jnp.cumsum(jax.nn.softmax(c_lb.astype(jnp.float32), axis=0), axis=0)
    lb_all = lb_all - lb_all[0:1]
    new_a, new_b, new_c, new_f = [], [], [], []
    for layer in range(DEPTH):
        kind, j = layer % N_MIXERS, layer // N_MIXERS
        h = rmsnorm(x, norm_mix[layer])
        if kind == 0:
            y, nb = conformer_conv_mixer(h, st_a[j], a_w_pw1[j], a_w_dw[j], a_b_dw[j], a_ln_g[j], a_ln_b[j], a_w_pw2[j])
            new_a.append(nb)
        elif kind == 1:
            y, nb = pool_mixer(h, st_b[j], start, b_w_grp[j], b_scale[j])
            new_b.append(nb)
        else:
            y, nb = hgrn2_mixer(h, st_c[j], lb_all[layer], c_w_in[j], c_g_norm[j], c_w_o[j])
            new_c.append(nb)
        x = x + y
        h = rmsnorm(x, norm_ffn[layer])
        y, nb = conv_ffn(h, st_f[layer], f_w_up[layer], f_w_dw[layer], f_b_dw[layer], f_w_down[layer])
        new_f.append(nb)
        x = x + y
    return (rmsnorm(x, norm_final), jnp.stack(new_a), jnp.stack(new_b), jnp.stack(new_c), jnp.stack(new_f))


def setup_inputs(seed: int = 0) -> dict:
    key = jax.random.key(seed)
    ks = jax.random.split(key, 32)
    nrm = lambda k, s, sc: jax.random.normal(k, s, jnp.float32) * sc
    D = D_MODEL
    return {
        "x_prompt": nrm(ks[0], (BATCH, SEQ, D), 1.0),
        "x_sample": nrm(ks[1], (DEC_BATCH, DEC_SEQ, D), 1.0),
        "state_conv_a": nrm(ks[2], (N_A, DEC_BATCH, CONV_W - 1, D_CONV), 0.5),
        "state_pool": nrm(ks[3], (N_B, DEC_BATCH, POOL_HIST, D), 1.0),
        "state_hgrn": nrm(ks[4], (N_C, DEC_BATCH, HGRN_HEADS, HGRN_DK, HGRN_DV), 0.5),
        "state_ffn_conv": nrm(ks[5], (DEPTH, DEC_BATCH, FFN_CONV_W - 1, 2 * D_FF), 1.0),
        "norm_mix": 1.0 + nrm(ks[6], (DEPTH, D), 0.02),
        "norm_ffn": 1.0 + nrm(ks[7], (DEPTH, D), 0.02),
        "norm_final": 1.0 + nrm(ks[8], (D,), 0.02),
        "a_w_pw1": nrm(ks[9], (N_A, D, 2 * D_CONV), D ** -0.5),
        "a_w_dw": nrm(ks[10], (N_A, CONV_W, D_CONV), CONV_W ** -0.5),
        "a_b_dw": nrm(ks[11], (N_A, D_CONV), 0.02),
        "a_ln_g": 1.0 + nrm(ks[12], (N_A, D_CONV), 0.02),
        "a_ln_b": nrm(ks[13], (N_A, D_CONV), 0.02),
        "a_w_pw2": nrm(ks[14], (N_A, D_CONV, D), D_CONV ** -0.5),
        "b_w_grp": nrm(ks[15], (N_B, N_POOL_GROUPS, POOL_GROUP, POOL_GROUP), POOL_GROUP ** -0.5),
        "b_scale": 1.0 + nrm(ks[16], (N_B, D), 0.02),
        "c_lb": nrm(ks[17], (DEPTH, HGRN_KD), 1.0),
        "c_w_in": nrm(ks[18], (N_C, D, 2 * HGRN_KD + 2 * D), D ** -0.5),
        "c_g_norm": 1.0 + nrm(ks[19], (N_C, D), 0.02),
        "c_w_o": nrm(ks[20], (N_C, D, D), D ** -0.5),
        "f_w_up": nrm(ks[21], (DEPTH, D, 2 * D_FF), D ** -0.5),
        "f_w_dw": nrm(ks[22], (DEPTH, FFN_CONV_W, 2 * D_FF), FFN_CONV_W ** -0.5),
        "f_b_dw": nrm(ks[23], (DEPTH, 2 * D_FF), 0.02),
        "f_w_down": nrm(ks[24], (DEPTH, D_FF, D), D_FF ** -0.5),
    }


def reference(x_prompt, x_sample, state_conv_a, state_pool, state_hgrn, state_ffn_conv,
              norm_mix, norm_ffn, norm_final, a_w_pw1, a_w_dw, a_b_dw, a_ln_g, a_ln_b, a_w_pw2,
              b_w_grp, b_scale, c_lb, c_w_in, c_g_norm, c_w_o, f_w_up, f_w_dw, f_b_dw, f_w_down):
    weights = (norm_mix, norm_ffn, norm_final, a_w_pw1, a_w_dw, a_b_dw, a_ln_g, a_ln_b, a_w_pw2,
               b_w_grp, b_scale, c_lb, c_w_in, c_g_norm, c_w_o, f_w_up, f_w_dw, f_b_dw, f_w_down)
    dt = x_prompt.dtype
    z_a = jnp.zeros((N_A, BATCH, CONV_W - 1, D_CONV), dt)
    z_b = jnp.zeros((N_B, BATCH, POOL_HIST, D_MODEL), dt)
    z_c = jnp.zeros((N_C, BATCH, HGRN_HEADS, HGRN_DK, HGRN_DV), dt)
    z_f = jnp.zeros((DEPTH, BATCH, FFN_CONV_W - 1, 2 * D_FF), dt)
    y_prompt, pa, pb, pc, pf = trunk(x_prompt, z_a, z_b, z_c, z_f, 0, *weights)
    y_sample, sa, sb, sc, sf = trunk(x_sample, state_conv_a, state_pool, state_hgrn, state_ffn_conv, PAST_LEN, *weights)
    return (y_prompt, y_sample, pa, sa, pb, sb, pc, sc, pf, sf)
```

```python
import functools

import jax
import jax.numpy as jnp
from jax import lax
from jax.experimental import pallas as pl
from jax.experimental.pallas import tpu as pltpu

F32 = jnp.float32
BF16 = jnp.bfloat16

NORM_EPS = 1e-6
POOL_WINDOWS = (2, 4, 8, 16)
HGRN_DK = 128
LANES = 128
SUBLANES = 8
HIST_PAD = 32
FFN_HIST_PAD = 8
VMEM_LIMIT_BYTES = 56 * 1024 * 1024
PAST_LEN = 16384
ROWS_MM = 1024
COLS_MM = 512
ROWS_CONV = 512
ROWS_FFN = 1024


def _params(n_axes):
    return pltpu.CompilerParams(dimension_semantics=("arbitrary",) * n_axes,
                                vmem_limit_bytes=VMEM_LIMIT_BYTES)


def _tile(n, pref, mult):
    t = min(pref, n)
    t -= t % mult
    while t >= mult:
        if n % t == 0:
            return t
        t -= mult
    return n


def _sigmoid(x):
    return 1.0 / (1.0 + jnp.exp(-x))


def _silu(x):
    return x * _sigmoid(x)


def _rms_rows(x, g):
    ms = jnp.mean(x * x, axis=-1, keepdims=True)
    return x * lax.rsqrt(ms + NORM_EPS) * g


def _ln_silu_rows(x, g, b):
    mu = jnp.mean(x, axis=-1, keepdims=True)
    d = x - mu
    var = jnp.mean(d * d, axis=-1, keepdims=True)
    return _silu(d * lax.rsqrt(var + NORM_EPS) * g + b)


def _row_chunks(tm, fn):
    rc = _tile(tm, 128, SUBLANES)

    def body(i, carry):
        fn(pl.ds(pl.multiple_of(i * rc, rc), rc))
        return carry

    lax.fori_loop(0, tm // rc, body, 0)


def _rms_kernel(x_ref, g_ref, o_ref):
    tm = x_ref.shape[0]

    def chunk(rows):
        o_ref[rows, :] = _rms_rows(x_ref[rows, :], g_ref[...])

    _row_chunks(tm, chunk)


def _rms(x, g):
    m, d = x.shape
    tm = _tile(m, 512, SUBLANES)
    return pl.pallas_call(
        _rms_kernel,
        grid=(m // tm,),
        in_specs=[pl.BlockSpec((tm, d), lambda i: (i, 0)),
                  pl.BlockSpec((1, d), lambda i: (0, 0))],
        out_specs=pl.BlockSpec((tm, d), lambda i: (i, 0)),
        out_shape=jax.ShapeDtypeStruct((m, d), F32),
        compiler_params=_params(1),
        name="rms",
    )(x, g.reshape(1, d))


def _pro_mm_kernel(*refs, mode, glu, has_res):
    it = iter(refs)
    x_ref = next(it)
    p0_ref = next(it) if mode in ("rms", "ln_silu") else None
    p1_ref = next(it) if mode == "ln_silu" else None
    w_ref = next(it)
    wg_ref = next(it) if glu else None
    res_ref = next(it) if has_res else None
    o_ref = next(it)
    tm = x_ref.shape[0]

    if x_ref.dtype == BF16:
        a = x_ref[...]
    else:
        h_ref = next(it)

        @pl.when(pl.program_id(1) == 0)
        def _():
            def chunk(rows):
                xv = x_ref[rows, :]
                if mode == "rms":
                    xv = _rms_rows(xv, p0_ref[...])
                elif mode == "ln_silu":
                    xv = _ln_silu_rows(xv, p0_ref[...], p1_ref[...])
                h_ref[rows, :] = xv.astype(BF16)

            _row_chunks(tm, chunk)

        a = h_ref[...]
    acc = jnp.dot(a, w_ref[...], preferred_element_type=F32)
    if glu:
        acc = acc * _sigmoid(jnp.dot(a, wg_ref[...], preferred_element_type=F32))
    if has_res:
        acc = res_ref[...] + acc
    o_ref[...] = acc


def _pro_mm(x, w, *, mode, p0=None, p1=None, glu=False, res=None, name):
    m, k = x.shape
    n = w.shape[1] // (2 if glu else 1)
    tm = _tile(m, ROWS_MM, 16)
    tn = _tile(n, COLS_MM, LANES)
    nn = n // tn
    args = [x]
    in_specs = [pl.BlockSpec((tm, k), lambda i, j: (i, 0))]
    for p in (p0, p1):
        if p is not None:
            args.append(p.reshape(1, k))
            in_specs.append(pl.BlockSpec((1, k), lambda i, j: (0, 0)))
    args.append(w)
    in_specs.append(pl.BlockSpec((k, tn), lambda i, j: (0, j)))
    if glu:
        args.append(w)
        in_specs.append(pl.BlockSpec((k, tn), lambda i, j: (0, j + nn)))
    if res is not None:
        args.append(res)
        in_specs.append(pl.BlockSpec((tm, tn), lambda i, j: (i, j)))
    return pl.pallas_call(
        functools.partial(_pro_mm_kernel, mode=mode, glu=glu, has_res=res is not None),
        grid=(m // tm, nn),
        in_specs=in_specs,
        out_specs=pl.BlockSpec((tm, tn), lambda i, j: (i, j)),
        out_shape=jax.ShapeDtypeStruct((m, n), F32),
        scratch_shapes=[] if x.dtype == BF16 else [pltpu.VMEM((tm, k), BF16)],
        compiler_params=_params(2),
        name=name,
    )(*args)


def _conv_long_kernel(v_ref, st_ref, w_ref, b_ref, c_ref, nb_ref, xp_ref, *, tps):
    tm = v_ref.shape[0]
    width = w_ref.shape[0]
    off = HIST_PAD - (width - 1)
    rc = _tile(tm, 128, SUBLANES)
    m = pl.program_id(1)
    start = (m % tps) == 0

    @pl.when(start)
    def _():
        xp_ref[0:HIST_PAD, :] = st_ref[0]

    @pl.when(jnp.logical_not(start))
    def _():
        xp_ref[0:HIST_PAD, :] = xp_ref[tm:tm + HIST_PAD, :]

    xp_ref[HIST_PAD:HIST_PAD + tm, :] = v_ref[...]
    for r in range(tm // rc):
        acc = jnp.broadcast_to(b_ref[...], (rc, LANES))
        for k in range(width):
            lo = r * rc + off + k
            acc = acc + w_ref[k:k + 1, :] * xp_ref[lo:lo + rc, :]
        c_ref[r * rc:(r + 1) * rc, :] = acc
    nb_ref[0] = xp_ref[tm:tm + HIST_PAD, :]


def _conv_long(v, st, w_dw, b_dw, *, n_seq):
    m, c = v.shape
    width = w_dw.shape[0]
    t = m // n_seq
    tm = _tile(t, ROWS_CONV, SUBLANES)
    assert tm >= HIST_PAD
    tps = t // tm
    return pl.pallas_call(
        functools.partial(_conv_long_kernel, tps=tps),
        grid=(c // LANES, m // tm),
        in_specs=[pl.BlockSpec((tm, LANES), lambda j, i: (i, j)),
                  pl.BlockSpec((1, HIST_PAD, LANES), lambda j, i: (i // tps, 0, j)),
                  pl.BlockSpec((width, LANES), lambda j, i: (0, j)),
                  pl.BlockSpec((1, LANES), lambda j, i: (0, j))],
        out_specs=[pl.BlockSpec((tm, LANES), lambda j, i: (i, j)),
                   pl.BlockSpec((1, HIST_PAD, LANES), lambda j, i: (i // tps, 0, j))],
        out_shape=[jax.ShapeDtypeStruct((m, c), F32),
                   jax.ShapeDtypeStruct((n_seq, HIST_PAD, c), F32)],
        scratch_shapes=[pltpu.VMEM((HIST_PAD + tm, LANES), F32)],
        compiler_params=_params(2),
        name="conv_long",
    )(v, st, w_dw, b_dw.reshape(1, c))


def _conv_slab_kernel(*refs, hist, n_tok, s):
    v_ref = refs[0]
    st_refs = refs[1:1 + hist]
    w_ref, b_ref, c_ref, nb_ref = refs[1 + hist:]
    width = w_ref.shape[0]

    def slab(i):
        if i < hist:
            return st_refs[i][...]
        return v_ref[(i - hist) * s:(i - hist + 1) * s, :]

    for t in range(n_tok):
        acc = jnp.broadcast_to(b_ref[...], (s, LANES))
        for k in range(width):
            acc = acc + w_ref[k:k + 1, :] * slab(t + k)
        c_ref[t * s:(t + 1) * s, :] = acc
    for j in range(hist):
        nb_ref[j] = slab(j + n_tok)


def _conv_slab(v, st2d, w_dw, b_dw, *, s):
    m, c = v.shape
    width = w_dw.shape[0]
    hist = width - 1
    n_tok = m // s
    ncb = c // LANES
    in_specs = [pl.BlockSpec((m, LANES), lambda j: (0, j))]
    in_specs += [pl.BlockSpec((s, LANES), lambda j, jj=jj: (0, jj * ncb + j)) for jj in range(hist)]
    in_specs += [pl.BlockSpec((width, LANES), lambda j: (0, j)),
                 pl.BlockSpec((1, LANES), lambda j: (0, j))]
    return pl.pallas_call(
        functools.partial(_conv_slab_kernel, hist=hist, n_tok=n_tok, s=s),
        grid=(ncb,),
        in_specs=in_specs,
        out_specs=[pl.BlockSpec((m, LANES), lambda j: (0, j)),
                   pl.BlockSpec((hist, s, LANES), lambda j: (0, 0, j))],
        out_shape=[jax.ShapeDtypeStruct((m, c), F32),
                   jax.ShapeDtypeStruct((hist, s, c), F32)],
        compiler_params=_params(1),
        name="conv_slab",
    )(v, *([st2d] * hist), w_dw, b_dw.reshape(1, c))


def _select_by_group(gid, vals):
    out = vals[-1]
    for g in range(len(vals) - 2, -1, -1):
        out = jnp.where(gid == g, vals[g], out)
    return out


def _pool_long_kernel(h_ref, st_ref, p_ref, nb_ref, xp_ref, s1_ref, s2_ref, s3_ref, s4_ref,
                      *, tps, start, lanes_per_group):
    tm = h_ref.shape[0]
    n = HIST_PAD + tm
    gid = pl.program_id(0) // lanes_per_group
    m = pl.program_id(1)
    first = (m % tps) == 0

    @pl.when(first)
    def _():
        xp_ref[0:HIST_PAD, :] = st_ref[0]

    @pl.when(jnp.logical_not(first))
    def _():
        xp_ref[0:HIST_PAD, :] = xp_ref[tm:tm + HIST_PAD, :]

    xp_ref[HIST_PAD:n, :] = h_ref[...]
    s1_ref[8:n, :] = xp_ref[8:n, :] + xp_ref[7:n - 1, :]
    s2_ref[16:n, :] = s1_ref[16:n, :] + s1_ref[14:n - 2, :]
    s3_ref[24:n, :] = s2_ref[24:n, :] + s2_ref[20:n - 4, :]
    s4_ref[32:n, :] = s3_ref[32:n, :] + s3_ref[24:n - 8, :]
    sums = [r[HIST_PAD:n, :] for r in (s1_ref, s2_ref, s3_ref, s4_ref)]
    if start >= max(POOL_WINDOWS) - 1:
        means = [sm * (1.0 / w) for sm, w in zip(sums, POOL_WINDOWS)]
    else:
        pos = start + (m % tps) * tm + lax.broadcasted_iota(jnp.int32, (tm, LANES), 0)
        means = [sm / jnp.minimum(w, pos + 1).astype(F32) for sm, w in zip(sums, POOL_WINDOWS)]
    p_ref[...] = (_select_by_group(gid, means) - h_ref[...]).astype(p_ref.dtype)
    nb_ref[0] = xp_ref[tm:tm + HIST_PAD, :]


def _pool_long(h, st, *, n_seq, start):
    m, d = h.shape
    t = m // n_seq
    tm = _tile(t, ROWS_CONV, 16)
    assert tm >= HIST_PAD
    tps = t // tm
    lanes_per_group = d // len(POOL_WINDOWS) // LANES
    return pl.pallas_call(
        functools.partial(_pool_long_kernel, tps=tps, start=start, lanes_per_group=lanes_per_group),
        grid=(d // LANES, m // tm),
        in_specs=[pl.BlockSpec((tm, LANES), lambda j, i: (i, j)),
                  pl.BlockSpec((1, HIST_PAD, LANES), lambda j, i: (i // tps, 0, j))],
        out_specs=[pl.BlockSpec((tm, LANES), lambda j, i: (i, j)),
                   pl.BlockSpec((1, HIST_PAD, LANES), lambda j, i: (i // tps, 0, j))],
        out_shape=[jax.ShapeDtypeStruct((m, d), BF16),
                   jax.ShapeDtypeStruct((n_seq, HIST_PAD, d), F32)],
        scratch_shapes=[pltpu.VMEM((HIST_PAD + tm, LANES), F32) for _ in range(5)],
        compiler_params=_params(2),
        name="pool_long",
    )(h, st)


def _pool_slab_kernel(*refs, hist, n_tok, s, start, lanes_per_group):
    h_ref = refs[0]
    st_refs = refs[1:1 + hist]
    p_ref, nb_ref = refs[1 + hist:]
    gid = pl.program_id(0) // lanes_per_group

    def slab(i):
        if i < hist:
            return st_refs[i][...]
        return h_ref[(i - hist) * s:(i - hist + 1) * s, :]

    n = hist + n_tok
    level = [slab(i) for i in range(n)]
    levels = []
    step = 1
    for _ in POOL_WINDOWS:
        level = [level[i] + level[i - step] if i >= 2 * step - 1 else None for i in range(n)]
        levels.append(level)
        step *= 2
    for t in range(n_tok):
        means = [lv[hist + t] * (1.0 / min(w, start + t + 1)) for lv, w in zip(levels, POOL_WINDOWS)]
        p_ref[t * s:(t + 1) * s, :] = (_select_by_group(gid, means) - slab(hist + t)).astype(p_ref.dtype)
    for j in range(hist):
        nb_ref[j] = slab(j + n_tok)


def _pool_slab(h, st2d, *, s, start):
    m, d = h.shape
    hist = max(POOL_WINDOWS) - 1
    n_tok = m // s
    ncb = d // LANES
    lanes_per_group = d // len(POOL_WINDOWS) // LANES
    in_specs = [pl.BlockSpec((m, LANES), lambda j: (0, j))]
    in_specs += [pl.BlockSpec((s, LANES), lambda j, jj=jj: (0, jj * ncb + j)) for jj in range(hist)]
    return pl.pallas_call(
        functools.partial(_pool_slab_kernel, hist=hist, n_tok=n_tok, s=s, start=start,
                          lanes_per_group=lanes_per_group),
        grid=(ncb,),
        in_specs=in_specs,
        out_specs=[pl.BlockSpec((m, LANES), lambda j: (0, j)),
                   pl.BlockSpec((hist, s, LANES), lambda j: (0, 0, j))],
        out_shape=[jax.ShapeDtypeStruct((m, d), BF16),
                   jax.ShapeDtypeStruct((hist, s, d), F32)],
        compiler_params=_params(1),
        name="pool_slab",
    )(h, *([st2d] * hist))


def _grp_mm_kernel(p_ref, w_ref, sc_ref, res_ref, o_ref):
    y = jnp.dot(p_ref[...], w_ref[0], preferred_element_type=F32)
    o_ref[...] = res_ref[...] + y * sc_ref[...]


def _grp_mm(p, w_grp, scale, res):
    m, d = p.shape
    ng, gc, _ = w_grp.shape
    tm = _tile(m, 1024, 16)
    return pl.pallas_call(
        _grp_mm_kernel,
        grid=(m // tm, ng),
        in_specs=[pl.BlockSpec((tm, gc), lambda i, g: (i, g)),
                  pl.BlockSpec((1, gc, gc), lambda i, g: (g, 0, 0)),
                  pl.BlockSpec((1, gc), lambda i, g: (0, g)),
                  pl.BlockSpec((tm, gc), lambda i, g: (i, g))],
        out_specs=pl.BlockSpec((tm, gc), lambda i, g: (i, g)),
        out_shape=jax.ShapeDtypeStruct((m, d), F32),
        compiler_params=_params(2),
        name="grp_mm",
    )(p, w_grp, scale.reshape(1, d), res)


def _cumsum_rows(x):
    n = x.shape[0]
    row = lax.broadcasted_iota(jnp.int32, x.shape, 0)
    d = 1
    while d < n:
        x = x + jnp.where(row >= d, pltpu.roll(x, d, axis=0), 0.0)
        d *= 2
    return x


def _hgrn_chunk(q_raw, fz, inp, lb, s0):
    c = q_raw.shape[0]
    f = lb + (1.0 - lb) * _sigmoid(fz)
    kk = 1.0 - f
    q = _silu(q_raw)
    g_cum = _cumsum_rows(jnp.log(f))
    o = jnp.dot(q * jnp.exp(g_cum), s0, preferred_element_type=F32)
    o_blk = [o[i * 8:(i + 1) * 8] for i in range(c // 8)]
    length = c // 2
    while length >= 8:
        for j in range(c // (2 * length)):
            s_lo = 2 * length * j
            mid = s_lo + length
            ref = g_cum[mid - 1:mid, :]
            qt = q[mid:mid + length] * jnp.exp(g_cum[mid:mid + length] - ref)
            kt = kk[s_lo:mid] * jnp.exp(ref - g_cum[s_lo:mid])
            sc = lax.dot_general(qt, kt, (((1,), (1,)), ((), ())), preferred_element_type=F32)
            add = jnp.dot(sc, inp[s_lo:mid], preferred_element_type=F32)
            for i in range(length // 8):
                o_blk[mid // 8 + i] = o_blk[mid // 8 + i] + add[i * 8:(i + 1) * 8]
        length //= 2
    nb = c // 8
    g3 = g_cum.reshape(nb, 8, HGRN_DK)
    q3 = q.reshape(nb, 8, HGRN_DK)
    k3 = kk.reshape(nb, 8, HGRN_DK)
    i3 = inp.reshape(nb, 8, HGRN_DK)
    t_idx = lax.broadcasted_iota(jnp.int32, (nb, 8, HGRN_DK), 1)
    ps = []
    for s in range(8):
        d = jnp.where(t_idx >= s, g3 - g3[:, s:s + 1, :], -jnp.inf)
        ps.append((q3 * jnp.exp(d) * k3[:, s:s + 1, :]).reshape(c, HGRN_DK))
    p_all = jnp.concatenate(ps, axis=0)
    a_all = jnp.dot(p_all, jnp.ones((HGRN_DK, LANES), F32), preferred_element_type=F32)
    o3 = jnp.concatenate(o_blk, axis=0).reshape(nb, 8, HGRN_DK)
    for s in range(8):
        o3 = o3 + a_all[s * c:(s + 1) * c].reshape(nb, 8, HGRN_DK) * i3[:, s:s + 1, :]
    g_end = g_cum[c - 1:c, :]
    kt = kk * jnp.exp(g_end - g_cum)
    upd = lax.dot_general(kt, inp, (((0,), (0,)), ((), ())), preferred_element_type=F32)
    decay = jnp.transpose(jnp.broadcast_to(jnp.exp(g_end), (HGRN_DK, HGRN_DK)))
    return o3.reshape(c, HGRN_DK), decay * s0 + upd


def _hgrn_kernel(q_ref, f_ref, i_ref, g_ref, clb_ref, gn_ref, s0_ref, o_ref, sf_ref,
                 *, layer, hu, chunk):
    @pl.when(pl.program_id(2) == 0)
    def _():
        sf_ref[...] = s0_ref[...]

    clb = clb_ref[...]
    e = jnp.exp(clb - jnp.max(clb, axis=0, keepdims=True))
    sm = e / jnp.sum(e, axis=0, keepdims=True)
    lb_all = jnp.zeros_like(sm[0:1])
    for i in range(1, layer + 1):
        lb_all = lb_all + sm[i:i + 1]
    tb = q_ref.shape[0]
    for h in range(hu):
        hs = slice(h * HGRN_DK, (h + 1) * HGRN_DK)
        for cix in range(tb // chunk):
            rows = slice(cix * chunk, (cix + 1) * chunk)
            o, s_new = _hgrn_chunk(q_ref[rows, hs], f_ref[rows, hs], i_ref[rows, hs],
                                   lb_all[:, hs], sf_ref[0, h])
            sf_ref[0, h] = s_new
            o = o * lax.rsqrt(jnp.mean(o * o, axis=-1, keepdims=True) + NORM_EPS)
            o_ref[rows, hs] = o * gn_ref[:, hs] * _silu(g_ref[rows, hs])


def _hgrn(proj, c_lb, g_norm, s0, *, n_seq, layer):
    m, d4 = proj.shape
    d = d4 // 4
    nh = d // HGRN_DK
    t = m // n_seq
    chunk = _tile(t, 128, SUBLANES)
    tb = _tile(t, 2 * chunk, chunk)
    hu = nh if t <= 16 else min(2, nh)
    nhg = nh // hu
    ntb = t // tb
    wblk = hu * HGRN_DK
    col = lambda part: (lambda b, hg, c: (b * ntb + c, part * nhg + hg))
    return pl.pallas_call(
        functools.partial(_hgrn_kernel, layer=layer, hu=hu, chunk=chunk),
        grid=(n_seq, nhg, ntb),
        in_specs=[pl.BlockSpec((tb, wblk), col(0)),
                  pl.BlockSpec((tb, wblk), col(1)),
                  pl.BlockSpec((tb, wblk), col(2)),
                  pl.BlockSpec((tb, wblk), col(3)),
                  pl.BlockSpec((c_lb.shape[0], wblk), lambda b, hg, c: (0, hg)),
                  pl.BlockSpec((1, wblk), lambda b, hg, c: (0, hg)),
                  pl.BlockSpec((1, hu, HGRN_DK, HGRN_DK), lambda b, hg, c: (b, hg, 0, 0))],
        out_specs=[pl.BlockSpec((tb, wblk), lambda b, hg, c: (b * ntb + c, hg)),
                   pl.BlockSpec((1, hu, HGRN_DK, HGRN_DK), lambda b, hg, c: (b, hg, 0, 0))],
        out_shape=[jax.ShapeDtypeStruct((m, d), F32),
                   jax.ShapeDtypeStruct(s0.shape, F32)],
        compiler_params=_params(3),
        name="hgrn",
    )(proj, proj, proj, proj, c_lb, g_norm.reshape(1, d), s0)


def _ffn_norm(x_ref, g_ref, h_ref):
    def chunk(rows):
        h_ref[rows, :] = _rms_rows(x_ref[rows, :], g_ref[...]).astype(BF16)

    _row_chunks(x_ref.shape[0], chunk)


def _ffn_up_long_kernel(x_ref, g_ref, wa_ref, wb_ref, dwa_ref, dwb_ref, ba_ref, bb_ref,
                        sta_ref, stb_ref, act_ref, nba_ref, nbb_ref,
                        h_ref, xpa_ref, xpb_ref, cra_ref, crb_ref, *, tps):
    tm = x_ref.shape[0]
    pad = FFN_HIST_PAD
    m = pl.program_id(0)
    f = pl.program_id(1)
    first = (m % tps) == 0

    @pl.when(f == 0)
    def _():
        _ffn_norm(x_ref, g_ref, h_ref)

    def half(w_ref, dw_ref, b_ref, st_ref, nb_ref, xp_ref, cr_ref):
        u = jnp.dot(h_ref[...], w_ref[...], preferred_element_type=F32)

        @pl.when(first)
        def _():
            xp_ref[0:pad, :] = st_ref[0]

        @pl.when(jnp.logical_not(first))
        def _():
            xp_ref[0:pad, :] = cr_ref[f]

        xp_ref[pad:pad + tm, :] = u
        tail = xp_ref[tm:tm + pad, :]
        cr_ref[f] = tail
        nb_ref[0] = tail
        return (dw_ref[0:1, :] * xp_ref[pad - 2:pad - 2 + tm, :]
                + dw_ref[1:2, :] * xp_ref[pad - 1:pad - 1 + tm, :]
                + dw_ref[2:3, :] * u + b_ref[...])

    ca = half(wa_ref, dwa_ref, ba_ref, sta_ref, nba_ref, xpa_ref, cra_ref)
    cb = half(wb_ref, dwb_ref, bb_ref, stb_ref, nbb_ref, xpb_ref, crb_ref)
    act_ref[...] = (_silu(ca) * cb).astype(BF16)


def _ffn_up_long(x, g, w_up, w_dw, b_dw, st, *, n_seq):
    m, d = x.shape
    f2 = w_up.shape[1]
    ff = f2 // 2
    t = m // n_seq
    tm = _tile(t, ROWS_FFN, 16)
    tps = t // tm
    tf = _tile(ff, COLS_MM, LANES)
    nf = ff // tf
    pad = FFN_HIST_PAD
    lo = lambda i, j: (0, j)
    hi = lambda i, j: (0, j + nf)
    act, nba, nbb = pl.pallas_call(
        functools.partial(_ffn_up_long_kernel, tps=tps),
        grid=(m // tm, nf),
        in_specs=[pl.BlockSpec((tm, d), lambda i, j: (i, 0)),
                  pl.BlockSpec((1, d), lambda i, j: (0, 0)),
                  pl.BlockSpec((d, tf), lo), pl.BlockSpec((d, tf), hi),
                  pl.BlockSpec((3, tf), lo), pl.BlockSpec((3, tf), hi),
                  pl.BlockSpec((1, tf), lo), pl.BlockSpec((1, tf), hi),
                  pl.BlockSpec((1, pad, tf), lambda i, j: (i // tps, 0, j)),
                  pl.BlockSpec((1, pad, tf), lambda i, j: (i // tps, 0, j + nf))],
        out_specs=[pl.BlockSpec((tm, tf), lambda i, j: (i, j)),
                   pl.BlockSpec((1, pad, tf), lambda i, j: (i, 0, j)),
                   pl.BlockSpec((1, pad, tf), lambda i, j: (i, 0, j))],
        out_shape=[jax.ShapeDtypeStruct((m, ff), BF16),
                   jax.ShapeDtypeStruct((m // tm, pad, ff), F32),
                   jax.ShapeDtypeStruct((m // tm, pad, ff), F32)],
        scratch_shapes=[pltpu.VMEM((tm, d), BF16),
                        pltpu.VMEM((pad + tm, tf), F32), pltpu.VMEM((pad + tm, tf), F32),
                        pltpu.VMEM((nf, pad, tf), F32), pltpu.VMEM((nf, pad, tf), F32)],
        compiler_params=_params(2),
        name="ffn_up_long",
    )(x, g.reshape(1, d), w_up, w_up, w_dw, w_dw, b_dw.reshape(1, f2), b_dw.reshape(1, f2), st, st)
    last = slice(tps - 1, None, tps)
    return act, jnp.concatenate([nba[last, pad - 2:], nbb[last, pad - 2:]], axis=-1)


def _ffn_up_slab_kernel(x_ref, g_ref, wa_ref, wb_ref, dwa_ref, dwb_ref, ba_ref, bb_ref,
                        sa0_ref, sa1_ref, sb0_ref, sb1_ref,
                        act_ref, na0_ref, na1_ref, nb0_ref, nb1_ref,
                        h_ref, xpa_ref, xpb_ref, *, s):
    m = x_ref.shape[0]

    @pl.when(pl.program_id(0) == 0)
    def _():
        _ffn_norm(x_ref, g_ref, h_ref)

    def half(w_ref, dw_ref, b_ref, s0_ref, s1_ref, n0_ref, n1_ref, xp_ref):
        u = jnp.dot(h_ref[...], w_ref[...], preferred_element_type=F32)
        xp_ref[0:s, :] = s0_ref[...]
        xp_ref[s:2 * s, :] = s1_ref[...]
        xp_ref[2 * s:2 * s + m, :] = u
        n0_ref[...] = xp_ref[m:m + s, :]
        n1_ref[...] = xp_ref[m + s:m + 2 * s, :]
        return (dw_ref[0:1, :] * xp_ref[0:m, :] + dw_ref[1:2, :] * xp_ref[s:s + m, :]
                + dw_ref[2:3, :] * u + b_ref[...])

    ca = half(wa_ref, dwa_ref, ba_ref, sa0_ref, sa1_ref, na0_ref, na1_ref, xpa_ref)
    cb = half(wb_ref, dwb_ref, bb_ref, sb0_ref, sb1_ref, nb0_ref, nb1_ref, xpb_ref)
    act_ref[...] = (_silu(ca) * cb).astype(BF16)


def _ffn_up_slab(x, g, w_up, w_dw, b_dw, st2d, *, s):
    m, d = x.shape
    f2 = w_up.shape[1]
    ff = f2 // 2
    tf = _tile(ff, COLS_MM, LANES)
    nf = ff // tf
    lo = lambda j: (0, j)
    hi = lambda j: (0, j + nf)
    outs = pl.pallas_call(
        functools.partial(_ffn_up_slab_kernel, s=s),
        grid=(nf,),
        in_specs=[pl.BlockSpec((m, d), lambda j: (0, 0)),
                  pl.BlockSpec((1, d), lambda j: (0, 0)),
                  pl.BlockSpec((d, tf), lo), pl.BlockSpec((d, tf), hi),
                  pl.BlockSpec((3, tf), lo), pl.BlockSpec((3, tf), hi),
                  pl.BlockSpec((1, tf), lo), pl.BlockSpec((1, tf), hi),
                  pl.BlockSpec((s, tf), lo), pl.BlockSpec((s, tf), lambda j: (0, j + 2 * nf)),
                  pl.BlockSpec((s, tf), hi), pl.BlockSpec((s, tf), lambda j: (0, j + 3 * nf))],
        out_specs=[pl.BlockSpec((m, tf), lo)] + [pl.BlockSpec((s, tf), lo)] * 4,
        out_shape=[jax.ShapeDtypeStruct((m, ff), BF16)] + [jax.ShapeDtypeStruct((s, ff), F32)] * 4,
        scratch_shapes=[pltpu.VMEM((m, d), BF16),
                        pltpu.VMEM((2 * s + m, tf), F32), pltpu.VMEM((2 * s + m, tf), F32)],
        compiler_params=_params(1),
        name="ffn_up_slab",
    )(x, g.reshape(1, d), w_up, w_up, w_dw, w_dw, b_dw.reshape(1, f2), b_dw.reshape(1, f2),
      st2d, st2d, st2d, st2d)
    act, na0, na1, nb0, nb1 = outs
    new = jnp.stack([jnp.concatenate([na0, nb0], axis=-1), jnp.concatenate([na1, nb1], axis=-1)], axis=1)
    return act, new


def _front_pad(st, pad):
    return jnp.pad(st, ((0, 0), (pad - st.shape[1], 0), (0, 0)))


def _trunk(x, st_a, st_b, st_c, st_f, start, wts, *, slab):
    (norm_mix, norm_ffn, norm_final, a_w_pw1, a_w_dw, a_b_dw, a_ln_g, a_ln_b, a_w_pw2,
     b_w_grp, b_scale, c_lb, c_w_in, c_g_norm, c_w_o, f_w_up, f_w_dw, f_b_dw, f_w_down) = wts
    bsz, t, d = x.shape
    depth = norm_mix.shape[0]
    m = bsz * t
    if slab:
        to_rows = lambda a: a.transpose(1, 0, 2).reshape(m, a.shape[-1])
        from_rows = lambda a: a.reshape(t, bsz, a.shape[-1]).transpose(1, 0, 2)
    else:
        to_rows = lambda a: a.reshape(m, a.shape[-1])
        from_rows = lambda a: a.reshape(bsz, t, a.shape[-1])
    xr = to_rows(x)
    new_a, new_b, new_c, new_f = [], [], [], []
    for layer in range(depth):
        kind, j = layer % 3, layer // 3
        if kind == 0:
            v = _pro_mm(xr, a_w_pw1[j], mode="rms", p0=norm_mix[layer], glu=True, name="a_pw1")
            hist = a_w_dw.shape[1] - 1
            if slab:
                c_pre, nb = _conv_slab(v, st_a[j].reshape(bsz, -1), a_w_dw[j], a_b_dw[j], s=bsz)
                nb = nb.transpose(1, 0, 2)
            else:
                c_pre, nb = _conv_long(v, _front_pad(st_a[j], HIST_PAD), a_w_dw[j], a_b_dw[j], n_seq=bsz)
                nb = nb[:, HIST_PAD - hist:]
            new_a.append(nb)
            xr = _pro_mm(c_pre, a_w_pw2[j], mode="ln_silu", p0=a_ln_g[j], p1=a_ln_b[j], res=xr, name="a_pw2")
        elif kind == 1:
            h = _rms(xr, norm_mix[layer])
            hist = max(POOL_WINDOWS) - 1
            if slab:
                pooled, nb = _pool_slab(h, st_b[j].reshape(bsz, -1), s=bsz, start=start)
                nb = nb.transpose(1, 0, 2)
            else:
                pooled, nb = _pool_long(h, _front_pad(st_b[j], HIST_PAD), n_seq=bsz, start=start)
                nb = nb[:, HIST_PAD - hist:]
            new_b.append(nb)
            xr = _grp_mm(pooled, b_w_grp[j], b_scale[j], xr)
        else:
            proj = _pro_mm(xr, c_w_in[j], mode="rms", p0=norm_mix[layer], name="c_in")
            if slab:
                proj = from_rows(proj).reshape(m, proj.shape[-1])
            og, s_fin = _hgrn(proj, c_lb, c_g_norm[j], st_c[j], n_seq=bsz, layer=layer)
            if slab:
                og = to_rows(og.reshape(bsz, t, d))
            new_c.append(s_fin)
            xr = _pro_mm(og, c_w_o[j], mode="cast", res=xr, name="c_out")
        if slab:
            act, nb = _ffn_up_slab(xr, norm_ffn[layer], f_w_up[layer], f_w_dw[layer], f_b_dw[layer],
                                   st_f[layer].reshape(bsz, -1), s=bsz)
        else:
            act, nb = _ffn_up_long(xr, norm_ffn[layer], f_w_up[layer], f_w_dw[layer], f_b_dw[layer],
                                   _front_pad(st_f[layer], FFN_HIST_PAD), n_seq=bsz)
        new_f.append(nb)
        xr = _pro_mm(act, f_w_down[layer], mode="cast", res=xr, name="f_down")
    y = from_rows(_rms(xr, norm_final))
    return y, jnp.stack(new_a), jnp.stack(new_b), jnp.stack(new_c), jnp.stack(new_f)


def kernel(x_prompt, x_sample, state_conv_a, state_pool, state_hgrn, state_ffn_conv, norm_mix, norm_ffn, norm_final, a_w_pw1, a_w_dw, a_b_dw, a_ln_g, a_ln_b, a_w_pw2, b_w_grp, b_scale, c_lb, c_w_in, c_g_norm, c_w_o, f_w_up, f_w_dw, f_b_dw, f_w_down):
    bf = lambda w: w.astype(BF16)
    wts = (norm_mix, norm_ffn, norm_final, bf(a_w_pw1), a_w_dw, a_b_dw, a_ln_g, a_ln_b, bf(a_w_pw2),
           bf(b_w_grp), b_scale, c_lb, bf(c_w_in), c_g_norm, bf(c_w_o), bf(f_w_up), f_w_dw, f_b_dw,
           bf(f_w_down))
    bp = x_prompt.shape[0]
    zeros = lambda st: jnp.zeros((st.shape[0], bp) + st.shape[2:], st.dtype)
    yp, pa, pb, pc, pf = _trunk(x_prompt, zeros(state_conv_a), zeros(state_pool), zeros(state_hgrn),
                                zeros(state_ffn_conv), 0, wts, slab=False)
    ys, sa, sb, sc, sf = _trunk(x_sample, state_conv_a, state_pool, state_hgrn, state_ffn_conv,
                                PAST_LEN, wts, slab=True)
    return (yp, ys, pa, sa, pb, sb, pc, sc, pf, sf)
```

```python
import functools

import numpy as np
import jax
import jax.numpy as jnp
from jax import lax
from jax.experimental import pallas as pl
from jax.experimental.pallas import tpu as pltpu

F32 = jnp.float32
BF16 = jnp.bfloat16

NORM_EPS = 1e-6
POOL_WINDOWS = (2, 4, 8, 16)
HGRN_DK = 128
LANES = 128
SUBLANES = 8
HIST_PAD = 32
FFN_HIST_PAD = 8
VMEM_LIMIT_BYTES = 56 * 1024 * 1024
PAST_LEN = 16384
ROWS_MM = 1024
COLS_MM = 512
ROWS_CONV = 512
ROWS_FFN = 1024


def _params(n_axes):
    return pltpu.CompilerParams(dimension_semantics=("arbitrary",) * n_axes,
                                vmem_limit_bytes=VMEM_LIMIT_BYTES)


def _tile(n, pref, mult):
    t = min(pref, n)
    t -= t % mult
    while t >= mult:
        if n % t == 0:
            return t
        t -= mult
    return n


def _sigmoid(x):
    return 1.0 / (1.0 + jnp.exp(-x))


def _silu(x):
    return x * _sigmoid(x)


def _rms_rows(x, g):
    ms = jnp.mean(x * x, axis=-1, keepdims=True)
    return x * lax.rsqrt(ms + NORM_EPS) * g


def _ln_silu_rows(x, g, b):
    mu = jnp.mean(x, axis=-1, keepdims=True)
    d = x - mu
    var = jnp.mean(d * d, axis=-1, keepdims=True)
    return _silu(d * lax.rsqrt(var + NORM_EPS) * g + b)


def _row_chunks(tm, fn):
    rc = _tile(tm, 128, SUBLANES)

    def body(i, carry):
        fn(pl.ds(pl.multiple_of(i * rc, rc), rc))
        return carry

    lax.fori_loop(0, tm // rc, body, 0)


def _rms_kernel(x_ref, g_ref, o_ref):
    tm = x_ref.shape[0]

    def chunk(rows):
        o_ref[rows, :] = _rms_rows(x_ref[rows, :], g_ref[...])

    _row_chunks(tm, chunk)


def _rms(x, g):
    m, d = x.shape
    tm = _tile(m, 512, SUBLANES)
    return pl.pallas_call(
        _rms_kernel,
        grid=(m // tm,),
        in_specs=[pl.BlockSpec((tm, d), lambda i: (i, 0)),
                  pl.BlockSpec((1, d), lambda i: (0, 0))],
        out_specs=pl.BlockSpec((tm, d), lambda i: (i, 0)),
        out_shape=jax.ShapeDtypeStruct((m, d), F32),
        compiler_params=_params(1),
        name="rms",
    )(x, g.reshape(1, d))


def _pro_mm_kernel(*refs, mode, glu, has_res):
    it = iter(refs)
    x_ref = next(it)
    p0_ref = next(it) if mode in ("rms", "ln_silu") else None
    p1_ref = next(it) if mode == "ln_silu" else None
    w_ref = next(it)
    wg_ref = next(it) if glu else None
    res_ref = next(it) if has_res else None
    o_ref = next(it)
    tm = x_ref.shape[0]

    if x_ref.dtype == BF16:
        a = x_ref[...]
    else:
        h_ref = next(it)

        @pl.when(pl.program_id(1) == 0)
        def _():
            def chunk(rows):
                xv = x_ref[rows, :]
                if mode == "rms":
                    xv = _rms_rows(xv, p0_ref[...])
                elif mode == "ln_silu":
                    xv = _ln_silu_rows(xv, p0_ref[...], p1_ref[...])
                h_ref[rows, :] = xv.astype(BF16)

            _row_chunks(tm, chunk)

        a = h_ref[...]
    acc = jnp.dot(a, w_ref[...], preferred_element_type=F32)
    if glu:
        acc = acc * _sigmoid(jnp.dot(a, wg_ref[...], preferred_element_type=F32))
    if has_res:
        acc = res_ref[...] + acc
    o_ref[...] = acc


def _pro_mm(x, w, layer, *, mode, p0=None, p1=None, glu=False, res=None, name):
    m, k = x.shape
    n = w.shape[2] // (2 if glu else 1)
    tm = _tile(m, ROWS_MM, 16)
    tn = _tile(n, COLS_MM, LANES)
    nn = n // tn
    args = [x]
    in_specs = [pl.BlockSpec((tm, k), lambda i, j: (i, 0))]
    for p in (p0, p1):
        if p is not None:
            args.append(p.reshape(1, k))
            in_specs.append(pl.BlockSpec((1, k), lambda i, j: (0, 0)))
    args.append(w)
    in_specs.append(pl.BlockSpec((None, k, tn), lambda i, j: (layer, 0, j)))
    if glu:
        args.append(w)
        in_specs.append(pl.BlockSpec((None, k, tn), lambda i, j: (layer, 0, j + nn)))
    if res is not None:
        args.append(res)
        in_specs.append(pl.BlockSpec((tm, tn), lambda i, j: (i, j)))
    return pl.pallas_call(
        functools.partial(_pro_mm_kernel, mode=mode, glu=glu, has_res=res is not None),
        grid=(m // tm, nn),
        in_specs=in_specs,
        out_specs=pl.BlockSpec((tm, tn), lambda i, j: (i, j)),
        out_shape=jax.ShapeDtypeStruct((m, n), F32),
        scratch_shapes=[] if x.dtype == BF16 else [pltpu.VMEM((tm, k), BF16)],
        compiler_params=_params(2),
        name=name,
    )(*args)


def _conv_long_kernel(v_ref, st_ref, w_ref, b_ref, c_ref, nb_ref, xp_ref, *, tps):
    tm = v_ref.shape[0]
    width = w_ref.shape[0]
    off = HIST_PAD - (width - 1)
    rc = _tile(tm, 128, SUBLANES)
    m = pl.program_id(1)
    start = (m % tps) == 0

    @pl.when(start)
    def _():
        xp_ref[0:HIST_PAD, :] = st_ref[0]

    @pl.when(jnp.logical_not(start))
    def _():
        xp_ref[0:HIST_PAD, :] = xp_ref[tm:tm + HIST_PAD, :]

    xp_ref[HIST_PAD:HIST_PAD + tm, :] = v_ref[...]
    for r in range(tm // rc):
        acc = jnp.broadcast_to(b_ref[...], (rc, LANES))
        for rho in range(min(SUBLANES, width)):
            taps = range(rho, width, SUBLANES)
            lo = r * rc + off + rho
            win = xp_ref[lo:lo + rc + SUBLANES * (len(taps) - 1), :]
            for i, k in enumerate(taps):
                acc = acc + w_ref[k:k + 1, :] * win[SUBLANES * i:SUBLANES * i + rc]
        c_ref[r * rc:(r + 1) * rc, :] = acc
    nb_ref[0] = xp_ref[tm:tm + HIST_PAD, :]


def _conv_long(v, st, w_dw, b_dw, *, n_seq):
    m, c = v.shape
    width = w_dw.shape[0]
    t = m // n_seq
    tm = _tile(t, ROWS_CONV, SUBLANES)
    assert tm >= HIST_PAD
    tps = t // tm
    return pl.pallas_call(
        functools.partial(_conv_long_kernel, tps=tps),
        grid=(c // LANES, m // tm),
        in_specs=[pl.BlockSpec((tm, LANES), lambda j, i: (i, j)),
                  pl.BlockSpec((1, HIST_PAD, LANES), lambda j, i: (i // tps, 0, j)),
                  pl.BlockSpec((width, LANES), lambda j, i: (0, j)),
                  pl.BlockSpec((1, LANES), lambda j, i: (0, j))],
        out_specs=[pl.BlockSpec((tm, LANES), lambda j, i: (i, j)),
                   pl.BlockSpec((1, HIST_PAD, LANES), lambda j, i: (i // tps, 0, j))],
        out_shape=[jax.ShapeDtypeStruct((m, c), F32),
                   jax.ShapeDtypeStruct((n_seq, HIST_PAD, c), F32)],
        scratch_shapes=[pltpu.VMEM((HIST_PAD + tm, LANES), F32)],
        compiler_params=_params(2),
        name="conv_long",
    )(v, st, w_dw, b_dw.reshape(1, c))


def _conv_slab_kernel(*refs, hist, n_tok, s):
    v_ref = refs[0]
    st_refs = refs[1:1 + hist]
    w_ref, b_ref, c_ref, nb_ref = refs[1 + hist:]
    width = w_ref.shape[0]

    def slab(i):
        if i < hist:
            return st_refs[i][...]
        return v_ref[(i - hist) * s:(i - hist + 1) * s, :]

    for t in range(n_tok):
        acc = jnp.broadcast_to(b_ref[...], (s, LANES))
        for k in range(width):
            acc = acc + w_ref[k:k + 1, :] * slab(t + k)
        c_ref[t * s:(t + 1) * s, :] = acc
    for j in range(hist):
        nb_ref[j] = slab(j + n_tok)


def _conv_slab(v, st2d, w_dw, b_dw, *, s):
    m, c = v.shape
    width = w_dw.shape[0]
    hist = width - 1
    n_tok = m // s
    ncb = c // LANES
    in_specs = [pl.BlockSpec((m, LANES), lambda j: (0, j))]
    in_specs += [pl.BlockSpec((s, LANES), lambda j, jj=jj: (0, jj * ncb + j)) for jj in range(hist)]
    in_specs += [pl.BlockSpec((width, LANES), lambda j: (0, j)),
                 pl.BlockSpec((1, LANES), lambda j: (0, j))]
    return pl.pallas_call(
        functools.partial(_conv_slab_kernel, hist=hist, n_tok=n_tok, s=s),
        grid=(ncb,),
        in_specs=in_specs,
        out_specs=[pl.BlockSpec((m, LANES), lambda j: (0, j)),
                   pl.BlockSpec((hist, s, LANES), lambda j: (0, 0, j))],
        out_shape=[jax.ShapeDtypeStruct((m, c), F32),
                   jax.ShapeDtypeStruct((hist, s, c), F32)],
        compiler_params=_params(1),
        name="conv_slab",
    )(v, *([st2d] * hist), w_dw, b_dw.reshape(1, c))


def _select_by_group(gid, vals):
    out = vals[-1]
    for g in range(len(vals) - 2, -1, -1):
        out = jnp.where(gid == g, vals[g], out)
    return out


def _pool_long_kernel(h_ref, st_ref, p_ref, nb_ref, xp_ref, s1_ref, s2_ref, s3_ref, s4_ref,
                      *, tps, start, lanes_per_group):
    tm = h_ref.shape[0]
    n = HIST_PAD + tm
    gid = pl.program_id(0) // lanes_per_group
    m = pl.program_id(1)
    first = (m % tps) == 0

    @pl.when(first)
    def _():
        xp_ref[0:HIST_PAD, :] = st_ref[0]

    @pl.when(jnp.logical_not(first))
    def _():
        xp_ref[0:HIST_PAD, :] = xp_ref[tm:tm + HIST_PAD, :]

    xp_ref[HIST_PAD:n, :] = h_ref[...]
    s1_ref[8:n, :] = xp_ref[8:n, :] + xp_ref[7:n - 1, :]
    s2_ref[16:n, :] = s1_ref[16:n, :] + s1_ref[14:n - 2, :]
    s3_ref[24:n, :] = s2_ref[24:n, :] + s2_ref[20:n - 4, :]
    s4_ref[32:n, :] = s3_ref[32:n, :] + s3_ref[24:n - 8, :]
    sums = [r[HIST_PAD:n, :] for r in (s1_ref, s2_ref, s3_ref, s4_ref)]
    if start >= max(POOL_WINDOWS) - 1:
        means = [sm * (1.0 / w) for sm, w in zip(sums, POOL_WINDOWS)]
    else:
        pos = start + (m % tps) * tm + lax.broadcasted_iota(jnp.int32, (tm, LANES), 0)
        means = [sm / jnp.minimum(w, pos + 1).astype(F32) for sm, w in zip(sums, POOL_WINDOWS)]
    p_ref[...] = (_select_by_group(gid, means) - h_ref[...]).astype(p_ref.dtype)
    nb_ref[0] = xp_ref[tm:tm + HIST_PAD, :]


def _pool_long(h, st, *, n_seq, start):
    m, d = h.shape
    t = m // n_seq
    tm = _tile(t, ROWS_CONV, 16)
    assert tm >= HIST_PAD
    tps = t // tm
    lanes_per_group = d // len(POOL_WINDOWS) // LANES
    return pl.pallas_call(
        functools.partial(_pool_long_kernel, tps=tps, start=start, lanes_per_group=lanes_per_group),
        grid=(d // LANES, m // tm),
        in_specs=[pl.BlockSpec((tm, LANES), lambda j, i: (i, j)),
                  pl.BlockSpec((1, HIST_PAD, LANES), lambda j, i: (i // tps, 0, j))],
        out_specs=[pl.BlockSpec((tm, LANES), lambda j, i: (i, j)),
                   pl.BlockSpec((1, HIST_PAD, LANES), lambda j, i: (i // tps, 0, j))],
        out_shape=[jax.ShapeDtypeStruct((m, d), BF16),
                   jax.ShapeDtypeStruct((n_seq, HIST_PAD, d), F32)],
        scratch_shapes=[pltpu.VMEM((HIST_PAD + tm, LANES), F32) for _ in range(5)],
        compiler_params=_params(2),
        name="pool_long",
    )(h, st)


def _pool_slab_kernel(*refs, hist, n_tok, s, start, lanes_per_group):
    h_ref = refs[0]
    st_refs = refs[1:1 + hist]
    p_ref, nb_ref = refs[1 + hist:]
    gid = pl.program_id(0) // lanes_per_group

    def slab(i):
        if i < hist:
            return st_refs[i][...]
        return h_ref[(i - hist) * s:(i - hist + 1) * s, :]

    n = hist + n_tok
    level = [slab(i) for i in range(n)]
    levels = []
    step = 1
    for _ in POOL_WINDOWS:
        level = [level[i] + level[i - step] if i >= 2 * step - 1 else None for i in range(n)]
        levels.append(level)
        step *= 2
    for t in range(n_tok):
        means = [lv[hist + t] * (1.0 / min(w, start + t + 1)) for lv, w in zip(levels, POOL_WINDOWS)]
        p_ref[t * s:(t + 1) * s, :] = (_select_by_group(gid, means) - slab(hist + t)).astype(p_ref.dtype)
    for j in range(hist):
        nb_ref[j] = slab(j + n_tok)


def _pool_slab(h, st2d, *, s, start):
    m, d = h.shape
    hist = max(POOL_WINDOWS) - 1
    n_tok = m // s
    ncb = d // LANES
    lanes_per_group = d // len(POOL_WINDOWS) // LANES
    in_specs = [pl.BlockSpec((m, LANES), lambda j: (0, j))]
    in_specs += [pl.BlockSpec((s, LANES), lambda j, jj=jj: (0, jj * ncb + j)) for jj in range(hist)]
    return pl.pallas_call(
        functools.partial(_pool_slab_kernel, hist=hist, n_tok=n_tok, s=s, start=start,
                          lanes_per_group=lanes_per_group),
        grid=(ncb,),
        in_specs=in_specs,
        out_specs=[pl.BlockSpec((m, LANES), lambda j: (0, j)),
                   pl.BlockSpec((hist, s, LANES), lambda j: (0, 0, j))],
        out_shape=[jax.ShapeDtypeStruct((m, d), BF16),
                   jax.ShapeDtypeStruct((hist, s, d), F32)],
        compiler_params=_params(1),
        name="pool_slab",
    )(h, *([st2d] * hist))


def _grp_mm_kernel(p_ref, w_ref, sc_ref, res_ref, o_ref):
    y = jnp.dot(p_ref[...], w_ref[0], preferred_element_type=F32)
    o_ref[...] = res_ref[...] + y * sc_ref[...]


def _grp_mm(p, w_grp, layer, scale, res):
    m, d = p.shape
    _, ng, gc, _ = w_grp.shape
    tm = _tile(m, ROWS_MM, 16)
    return pl.pallas_call(
        _grp_mm_kernel,
        grid=(m // tm, ng),
        in_specs=[pl.BlockSpec((tm, gc), lambda i, g: (i, g)),
                  pl.BlockSpec((None, 1, gc, gc), lambda i, g: (layer, g, 0, 0)),
                  pl.BlockSpec((1, gc), lambda i, g: (0, g)),
                  pl.BlockSpec((tm, gc), lambda i, g: (i, g))],
        out_specs=pl.BlockSpec((tm, gc), lambda i, g: (i, g)),
        out_shape=jax.ShapeDtypeStruct((m, d), F32),
        compiler_params=_params(2),
        name="grp_mm",
    )(p, w_grp, scale.reshape(1, d), res)


def _cumsum_rows(x):
    n = x.shape[0]
    row = lax.broadcasted_iota(jnp.int32, x.shape, 0)
    d = 1
    while d < n:
        x = x + jnp.where(row >= d, pltpu.roll(x, d, axis=0), 0.0)
        d *= 2
    return x


def _level_table(c):
    t = np.arange(c)[:, None]
    s = np.arange(c)[None, :]
    x = np.maximum(t ^ s, 1)
    top = np.left_shift(1, np.floor(np.log2(x)).astype(np.int64))
    return np.where(s < t, top, np.where(s == t, 0, -1)).astype(np.int32)


def _level_ref(g_cum, length):
    c = g_cum.shape[0]
    if length >= SUBLANES:
        parts = [jnp.broadcast_to(g_cum[b + length - 1:b + length, :], (2 * length, HGRN_DK))
                 for b in range(0, c, 2 * length)]
        return parts[0] if len(parts) == 1 else jnp.concatenate(parts, axis=0)
    g3 = g_cum.reshape(c // SUBLANES, SUBLANES, HGRN_DK)
    sub = lax.broadcasted_iota(jnp.int32, g3.shape, 1)
    ref = None
    for b in range(SUBLANES - 2 * length, -1, -2 * length):
        cand = jnp.broadcast_to(g3[:, b + length - 1:b + length, :], g3.shape)
        ref = cand if ref is None else jnp.where(sub < b + 2 * length, cand, ref)
    return ref.reshape(c, HGRN_DK)


def _hgrn_chunk(q_raw, fz, inp, lb, s0, lv):
    c = q_raw.shape[0]
    f = lb + (1.0 - lb) * _sigmoid(fz)
    kk = 1.0 - f
    q = _silu(q_raw)
    g_cum = _cumsum_rows(jnp.log(f))
    scores = jnp.zeros((c, c), F32)
    length = 0
    while length < c:
        if length == 0:
            qt, kt = q, kk
        else:
            e = jnp.exp(-jnp.abs(g_cum - _level_ref(g_cum, length)))
            qt, kt = q * e, kk * e
        sc = lax.dot_general(qt, kt, (((1,), (1,)), ((), ())), preferred_element_type=F32)
        scores = jnp.where(lv == length, sc, scores)
        length = max(1, 2 * length)
    o = (jnp.dot(q * jnp.exp(g_cum), s0, preferred_element_type=F32)
         + jnp.dot(scores, inp, preferred_element_type=F32))
    g_end = g_cum[c - 1:c, :]
    kt = kk * jnp.exp(g_end - g_cum)
    upd = lax.dot_general(kt, inp, (((0,), (0,)), ((), ())), preferred_element_type=F32)
    decay = jnp.transpose(jnp.broadcast_to(jnp.exp(g_end), (HGRN_DK, HGRN_DK)))
    return o, decay * s0 + upd


def _hgrn_kernel(q_ref, f_ref, i_ref, g_ref, clb_ref, gn_ref, lv_ref, s0_ref, o_ref, sf_ref,
                 *, layer, hu, chunk):
    @pl.when(pl.program_id(2) == 0)
    def _():
        sf_ref[...] = s0_ref[...]

    clb = clb_ref[...]
    e = jnp.exp(clb - jnp.max(clb, axis=0, keepdims=True))
    sm = e / jnp.sum(e, axis=0, keepdims=True)
    lb_all = jnp.zeros_like(sm[0:1])
    for i in range(1, layer + 1):
        lb_all = lb_all + sm[i:i + 1]
    tb = q_ref.shape[0]
    lv = lv_ref[...]
    for h in range(hu):
        hs = slice(h * HGRN_DK, (h + 1) * HGRN_DK)
        state = sf_ref[0, h]
        for cix in range(tb // chunk):
            rows = slice(cix * chunk, (cix + 1) * chunk)
            o, state = _hgrn_chunk(q_ref[rows, hs], f_ref[rows, hs], i_ref[rows, hs],
                                   lb_all[:, hs], state, lv)
            o = o * lax.rsqrt(jnp.mean(o * o, axis=-1, keepdims=True) + NORM_EPS)
            o_ref[rows, hs] = o * gn_ref[:, hs] * _silu(g_ref[rows, hs])
        sf_ref[0, h] = state


def _hgrn(proj, c_lb, g_norm, s0, *, n_seq, layer):
    m, d4 = proj.shape
    d = d4 // 4
    nh = d // HGRN_DK
    t = m // n_seq
    chunk = _tile(t, 128, SUBLANES)
    tb = _tile(t, 2 * chunk, chunk)
    hu = nh if t <= 16 else min(2, nh)
    nhg = nh // hu
    ntb = t // tb
    wblk = hu * HGRN_DK
    col = lambda part: (lambda b, hg, c: (b * ntb + c, part * nhg + hg))
    return pl.pallas_call(
        functools.partial(_hgrn_kernel, layer=layer, hu=hu, chunk=chunk),
        grid=(n_seq, nhg, ntb),
        in_specs=[pl.BlockSpec((tb, wblk), col(0)),
                  pl.BlockSpec((tb, wblk), col(1)),
                  pl.BlockSpec((tb, wblk), col(2)),
                  pl.BlockSpec((tb, wblk), col(3)),
                  pl.BlockSpec((c_lb.shape[0], wblk), lambda b, hg, c: (0, hg)),
                  pl.BlockSpec((1, wblk), lambda b, hg, c: (0, hg)),
                  pl.BlockSpec((chunk, chunk), lambda b, hg, c: (0, 0)),
                  pl.BlockSpec((1, hu, HGRN_DK, HGRN_DK), lambda b, hg, c: (b, hg, 0, 0))],
        out_specs=[pl.BlockSpec((tb, wblk), lambda b, hg, c: (b * ntb + c, hg)),
                   pl.BlockSpec((1, hu, HGRN_DK, HGRN_DK), lambda b, hg, c: (b, hg, 0, 0))],
        out_shape=[jax.ShapeDtypeStruct((m, d), F32),
                   jax.ShapeDtypeStruct(s0.shape, F32)],
        compiler_params=_params(3),
        name="hgrn",
    )(proj, proj, proj, proj, c_lb, g_norm.reshape(1, d), jnp.asarray(_level_table(chunk)), s0)


def _ffn_norm(x_ref, g_ref, h_ref):
    def chunk(rows):
        h_ref[rows, :] = _rms_rows(x_ref[rows, :], g_ref[...]).astype(BF16)

    _row_chunks(x_ref.shape[0], chunk)


def _ffn_up_long_kernel(x_ref, g_ref, wa_ref, wb_ref, dwa_ref, dwb_ref, ba_ref, bb_ref,
                        sta_ref, stb_ref, act_ref, nba_ref, nbb_ref,
                        h_ref, xpa_ref, xpb_ref, cra_ref, crb_ref, twa_ref, twb_ref, *, tps):
    tm = x_ref.shape[0]
    pad = FFN_HIST_PAD
    m = pl.program_id(0)
    f = pl.program_id(1)
    first = (m % tps) == 0

    @pl.when(f == 0)
    def _():
        _ffn_norm(x_ref, g_ref, h_ref)

    tf = act_ref.shape[1]
    cols = [slice(j * LANES, (j + 1) * LANES) for j in range(tf // LANES)]

    @pl.when(first)
    def _():
        for j, cs in enumerate(cols):
            xpa_ref[j, 0:pad, :] = sta_ref[0, :, cs]
            xpb_ref[j, 0:pad, :] = stb_ref[0, :, cs]

    @pl.when(jnp.logical_not(first))
    def _():
        for j, cs in enumerate(cols):
            xpa_ref[j, 0:pad, :] = cra_ref[f, :, cs]
            xpb_ref[j, 0:pad, :] = crb_ref[f, :, cs]

    rb = _tile(tm, max(tm // 4, 16), 16)
    rc = _tile(rb, 64, 16)

    def project(i):
        rows = slice(i * rb, (i + 1) * rb)
        for xp_ref, w_ref in ((xpa_ref, wa_ref), (xpb_ref, wb_ref)):
            u = jnp.dot(h_ref[rows, :], w_ref[...], preferred_element_type=F32)
            for j, cs in enumerate(cols):
                xp_ref[j, pad + i * rb:pad + (i + 1) * rb, :] = u[:, cs]

    for tw_ref, dw_ref, b_ref in ((twa_ref, dwa_ref, ba_ref), (twb_ref, dwb_ref, bb_ref)):
        for k in range(3):
            tw_ref[k] = jnp.broadcast_to(dw_ref[k:k + 1, :], (pad, tf))
        tw_ref[3] = jnp.broadcast_to(b_ref[...], (pad, tf))

    def conv(xp_ref, tw_ref, j, r0):
        def rows_back(k):
            return xp_ref[j, r0 + pad - k:r0 + pad - k + rc, :].reshape(rc // pad, pad, LANES)

        tap = lambda k: tw_ref[k, :, cols[j]]
        return tap(0) * rows_back(2) + tap(1) * rows_back(1) + tap(2) * rows_back(0) + tap(3)

    def gate(i):
        for r0 in range(i * rb, (i + 1) * rb, rc):
            for j, cs in enumerate(cols):
                ca = conv(xpa_ref, twa_ref, j, r0)
                cb = conv(xpb_ref, twb_ref, j, r0)
                act_ref[r0:r0 + rc, cs] = (_silu(ca) * cb).reshape(rc, LANES).astype(BF16)

    project(0)
    for i in range(1, tm // rb):
        project(i)
        gate(i - 1)
    gate(tm // rb - 1)
    for xp_ref, cr_ref, nb_ref in ((xpa_ref, cra_ref, nba_ref), (xpb_ref, crb_ref, nbb_ref)):
        for j, cs in enumerate(cols):
            tail = xp_ref[j, tm:tm + pad, :]
            cr_ref[f, :, cs] = tail
            nb_ref[0, :, cs] = tail


def _ffn_up_long(x, g, w_up, layer, w_dw, b_dw, st, *, n_seq):
    m, d = x.shape
    f2 = w_up.shape[2]
    ff = f2 // 2
    t = m // n_seq
    tm = _tile(t, ROWS_FFN, 16)
    tps = t // tm
    tf = _tile(ff, COLS_MM, LANES)
    nf = ff // tf
    pad = FFN_HIST_PAD
    lo = lambda i, j: (0, j)
    hi = lambda i, j: (0, j + nf)
    act, nba, nbb = pl.pallas_call(
        functools.partial(_ffn_up_long_kernel, tps=tps),
        grid=(m // tm, nf),
        in_specs=[pl.BlockSpec((tm, d), lambda i, j: (i, 0)),
                  pl.BlockSpec((1, d), lambda i, j: (0, 0)),
                  pl.BlockSpec((None, d, tf), lambda i, j: (layer, 0, j)),
                  pl.BlockSpec((None, d, tf), lambda i, j: (layer, 0, j + nf)),
                  pl.BlockSpec((3, tf), lo), pl.BlockSpec((3, tf), hi),
                  pl.BlockSpec((1, tf), lo), pl.BlockSpec((1, tf), hi),
                  pl.BlockSpec((1, pad, tf), lambda i, j: (i // tps, 0, j)),
                  pl.BlockSpec((1, pad, tf), lambda i, j: (i // tps, 0, j + nf))],
        out_specs=[pl.BlockSpec((tm, tf), lambda i, j: (i, j)),
                   pl.BlockSpec((1, pad, tf), lambda i, j: (i, 0, j)),
                   pl.BlockSpec((1, pad, tf), lambda i, j: (i, 0, j))],
        out_shape=[jax.ShapeDtypeStruct((m, ff), BF16),
                   jax.ShapeDtypeStruct((m // tm, pad, ff), F32),
                   jax.ShapeDtypeStruct((m // tm, pad, ff), F32)],
        scratch_shapes=[pltpu.VMEM((tm, d), BF16),
                        pltpu.VMEM((tf // LANES, pad + tm, LANES), F32),
                        pltpu.VMEM((tf // LANES, pad + tm, LANES), F32),
                        pltpu.VMEM((nf, pad, tf), F32), pltpu.VMEM((nf, pad, tf), F32),
                        pltpu.VMEM((4, pad, tf), F32), pltpu.VMEM((4, pad, tf), F32)],
        compiler_params=_params(2),
        name="ffn_up_long",
    )(x, g.reshape(1, d), w_up, w_up, w_dw, w_dw, b_dw.reshape(1, f2), b_dw.reshape(1, f2), st, st)
    last = slice(tps - 1, None, tps)
    return act, jnp.concatenate([nba[last, pad - 2:], nbb[last, pad - 2:]], axis=-1)


def _ffn_up_slab_kernel(x_ref, g_ref, wa_ref, wb_ref, dwa_ref, dwb_ref, ba_ref, bb_ref,
                        sa0_ref, sa1_ref, sb0_ref, sb1_ref,
                        act_ref, na0_ref, na1_ref, nb0_ref, nb1_ref,
                        h_ref, xpa_ref, xpb_ref, *, s):
    m = x_ref.shape[0]

    @pl.when(pl.program_id(0) == 0)
    def _():
        _ffn_norm(x_ref, g_ref, h_ref)

    def half(w_ref, dw_ref, b_ref, s0_ref, s1_ref, n0_ref, n1_ref, xp_ref):
        u = jnp.dot(h_ref[...], w_ref[...], preferred_element_type=F32)
        xp_ref[0:s, :] = s0_ref[...]
        xp_ref[s:2 * s, :] = s1_ref[...]
        xp_ref[2 * s:2 * s + m, :] = u
        n0_ref[...] = xp_ref[m:m + s, :]
        n1_ref[...] = xp_ref[m + s:m + 2 * s, :]
        return (dw_ref[0:1, :] * xp_ref[0:m, :] + dw_ref[1:2, :] * xp_ref[s:s + m, :]
                + dw_ref[2:3, :] * u + b_ref[...])

    ca = half(wa_ref, dwa_ref, ba_ref, sa0_ref, sa1_ref, na0_ref, na1_ref, xpa_ref)
    cb = half(wb_ref, dwb_ref, bb_ref, sb0_ref, sb1_ref, nb0_ref, nb1_ref, xpb_ref)
    act_ref[...] = (_silu(ca) * cb).astype(BF16)


def _ffn_up_slab(x, g, w_up, layer, w_dw, b_dw, st2d, *, s):
    m, d = x.shape
    f2 = w_up.shape[2]
    ff = f2 // 2
    tf = _tile(ff, COLS_MM, LANES)
    nf = ff // tf
    lo = lambda j: (0, j)
    hi = lambda j: (0, j + nf)
    outs = pl.pallas_call(
        functools.partial(_ffn_up_slab_kernel, s=s),
        grid=(nf,),
        in_specs=[pl.BlockSpec((m, d), lambda j: (0, 0)),
                  pl.BlockSpec((1, d), lambda j: (0, 0)),
                  pl.BlockSpec((None, d, tf), lambda j: (layer, 0, j)),
                  pl.BlockSpec((None, d, tf), lambda j: (layer, 0, j + nf)),
                  pl.BlockSpec((3, tf), lo), pl.BlockSpec((3, tf), hi),
                  pl.BlockSpec((1, tf), lo), pl.BlockSpec((1, tf), hi),
                  pl.BlockSpec((s, tf), lo), pl.BlockSpec((s, tf), lambda j: (0, j + 2 * nf)),
                  pl.BlockSpec((s, tf), hi), pl.BlockSpec((s, tf), lambda j: (0, j + 3 * nf))],
        out_specs=[pl.BlockSpec((m, tf), lo)] + [pl.BlockSpec((s, tf), lo)] * 4,
        out_shape=[jax.ShapeDtypeStruct((m, ff), BF16)] + [jax.ShapeDtypeStruct((s, ff), F32)] * 4,
        scratch_shapes=[pltpu.VMEM((m, d), BF16),
                        pltpu.VMEM((2 * s + m, tf), F32), pltpu.VMEM((2 * s + m, tf), F32)],
        compiler_params=_params(1),
        name="ffn_up_slab",
    )(x, g.reshape(1, d), w_up, w_up, w_dw, w_dw, b_dw.reshape(1, f2), b_dw.reshape(1, f2),
      st2d, st2d, st2d, st2d)
    act, na0, na1, nb0, nb1 = outs
    new = jnp.stack([jnp.concatenate([na0, nb0], axis=-1), jnp.concatenate([na1, nb1], axis=-1)], axis=1)
    return act, new


def _front_pad(st, pad):
    return jnp.pad(st, ((0, 0), (pad - st.shape[1], 0), (0, 0)))


def _trunk(x, st_a, st_b, st_c, st_f, start, wts, *, slab):
    (norm_mix, norm_ffn, norm_final, a_w_pw1, a_w_dw, a_b_dw, a_ln_g, a_ln_b, a_w_pw2,
     b_w_grp, b_scale, c_lb, c_w_in, c_g_norm, c_w_o, f_w_up, f_w_dw, f_b_dw, f_w_down) = wts
    bsz, t, d = x.shape
    depth = norm_mix.shape[0]
    m = bsz * t
    if slab:
        to_rows = lambda a: a.transpose(1, 0, 2).reshape(m, a.shape[-1])
        from_rows = lambda a: a.reshape(t, bsz, a.shape[-1]).transpose(1, 0, 2)
    else:
        to_rows = lambda a: a.reshape(m, a.shape[-1])
        from_rows = lambda a: a.reshape(bsz, t, a.shape[-1])
    xr = to_rows(x)
    new_a, new_b, new_c, new_f = [], [], [], []
    for layer in range(depth):
        kind, j = layer % 3, layer // 3
        if kind == 0:
            v = _pro_mm(xr, a_w_pw1, j, mode="rms", p0=norm_mix[layer], glu=True, name="a_pw1")
            hist = a_w_dw.shape[1] - 1
            if slab:
                c_pre, nb = _conv_slab(v, st_a[j].reshape(bsz, -1), a_w_dw[j], a_b_dw[j], s=bsz)
                nb = nb.transpose(1, 0, 2)
            else:
                c_pre, nb = _conv_long(v, _front_pad(st_a[j], HIST_PAD), a_w_dw[j], a_b_dw[j], n_seq=bsz)
                nb = nb[:, HIST_PAD - hist:]
            new_a.append(nb)
            xr = _pro_mm(c_pre, a_w_pw2, j, mode="ln_silu", p0=a_ln_g[j], p1=a_ln_b[j], res=xr, name="a_pw2")
        elif kind == 1:
            h = _rms(xr, norm_mix[layer])
            hist = max(POOL_WINDOWS) - 1
            if slab:
                pooled, nb = _pool_slab(h, st_b[j].reshape(bsz, -1), s=bsz, start=start)
                nb = nb.transpose(1, 0, 2)
            else:
                pooled, nb = _pool_long(h, _front_pad(st_b[j], HIST_PAD), n_seq=bsz, start=start)
                nb = nb[:, HIST_PAD - hist:]
            new_b.append(nb)
            xr = _grp_mm(pooled, b_w_grp, j, b_scale[j], xr)
        else:
            proj = _pro_mm(xr, c_w_in, j, mode="rms", p0=norm_mix[layer], name="c_in")
            if slab:
                proj = from_rows(proj).reshape(m, proj.shape[-1])
            og, s_fin = _hgrn(proj, c_lb, c_g_norm[j], st_c[j], n_seq=bsz, layer=layer)
            if slab:
                og = to_rows(og.reshape(bsz, t, d))
            new_c.append(s_fin)
            xr = _pro_mm(og, c_w_o, j, mode="cast", res=xr, name="c_out")
        if slab:
            act, nb = _ffn_up_slab(xr, norm_ffn[layer], f_w_up, layer, f_w_dw[layer], f_b_dw[layer],
                                   st_f[layer].reshape(bsz, -1), s=bsz)
        else:
            act, nb = _ffn_up_long(xr, norm_ffn[layer], f_w_up, layer, f_w_dw[layer], f_b_dw[layer],
                                   _front_pad(st_f[layer], FFN_HIST_PAD), n_seq=bsz)
        new_f.append(nb)
        xr = _pro_mm(act, f_w_down, layer, mode="cast", res=xr, name="f_down")
    y = from_rows(_rms(xr, norm_final))
    return y, jnp.stack(new_a), jnp.stack(new_b), jnp.stack(new_c), jnp.stack(new_f)


def kernel(x_prompt, x_sample, state_conv_a, state_pool, state_hgrn, state_ffn_conv, norm_mix, norm_ffn, norm_final, a_w_pw1, a_w_dw, a_b_dw, a_ln_g, a_ln_b, a_w_pw2, b_w_grp, b_scale, c_lb, c_w_in, c_g_norm, c_w_o, f_w_up, f_w_dw, f_b_dw, f_w_down):
    bf = lambda w: w.astype(BF16)
    wts = (norm_mix, norm_ffn, norm_final, bf(a_w_pw1), a_w_dw, a_b_dw, a_ln_g, a_ln_b, bf(a_w_pw2),
           bf(b_w_grp), b_scale, c_lb, bf(c_w_in), c_g_norm, bf(c_w_o), bf(f_w_up), f_w_dw, f_b_dw,
           bf(f_w_down))
    bp = x_prompt.shape[0]
    zeros = lambda st: jnp.zeros((st.shape[0], bp) + st.shape[2:], st.dtype)
    yp, pa, pb, pc, pf = _trunk(x_prompt, zeros(state_conv_a), zeros(state_pool), zeros(state_hgrn),
                                zeros(state_ffn_conv), 0, wts, slab=False)
    ys, sa, sb, sc, sf = _trunk(x_sample, state_conv_a, state_pool, state_hgrn, state_ffn_conv,
                                PAST_LEN, wts, slab=True)
    return (yp, ys, pa, sa, pb, sb, pc, sc, pf, sf)
```

```python
import functools

import numpy as np
import jax
import jax.numpy as jnp
from jax import lax
from jax.experimental import pallas as pl
from jax.experimental.pallas import tpu as pltpu

F32 = jnp.float32
BF16 = jnp.bfloat16

NORM_EPS = 1e-6
POOL_WINDOWS = (2, 4, 8, 16)
HGRN_DK = 128
LANES = 128
SUBLANES = 8
HIST_PAD = 32
FFN_HIST_PAD = 8
VMEM_LIMIT_BYTES = 56 * 1024 * 1024
PAST_LEN = 16384
ROWS_MM = 1024
COLS_MM = 1024
COLS_FFN = 512
MM_VMEM_BUDGET = 46 * 1024 * 1024
ROWS_CONV = 512
COLS_CONV = 512
ROWS_FFN = 1024


def _params(n_axes):
    return pltpu.CompilerParams(dimension_semantics=("arbitrary",) * n_axes,
                                vmem_limit_bytes=VMEM_LIMIT_BYTES)


def _tile(n, pref, mult):
    t = min(pref, n)
    t -= t % mult
    while t >= mult:
        if n % t == 0:
            return t
        t -= mult
    return n


def _sigmoid(x):
    return 1.0 / (1.0 + jnp.exp(-x))


def _silu(x):
    return x * _sigmoid(x)


def _rms_rows(x, g):
    ms = jnp.mean(x * x, axis=-1, keepdims=True)
    return x * lax.rsqrt(ms + NORM_EPS) * g


def _ln_silu_rows(x, g, b):
    mu = jnp.mean(x, axis=-1, keepdims=True)
    d = x - mu
    var = jnp.mean(d * d, axis=-1, keepdims=True)
    return _silu(d * lax.rsqrt(var + NORM_EPS) * g + b)


def _row_chunks(tm, fn):
    rc = _tile(tm, 128, SUBLANES)

    def body(i, carry):
        fn(pl.ds(pl.multiple_of(i * rc, rc), rc))
        return carry

    lax.fori_loop(0, tm // rc, body, 0)


def _rms_kernel(x_ref, g_ref, o_ref):
    tm = x_ref.shape[0]

    def chunk(rows):
        o_ref[rows, :] = _rms_rows(x_ref[rows, :], g_ref[...])

    _row_chunks(tm, chunk)


def _rms(x, g):
    m, d = x.shape
    tm = _tile(m, 512, SUBLANES)
    return pl.pallas_call(
        _rms_kernel,
        grid=(m // tm,),
        in_specs=[pl.BlockSpec((tm, d), lambda i: (i, 0)),
                  pl.BlockSpec((1, d), lambda i: (0, 0))],
        out_specs=pl.BlockSpec((tm, d), lambda i: (i, 0)),
        out_shape=jax.ShapeDtypeStruct((m, d), F32),
        compiler_params=_params(1),
        name="rms",
    )(x, g.reshape(1, d))


def _pro_mm_kernel(*refs, mode, glu, has_res):
    it = iter(refs)
    x_ref = next(it)
    p0_ref = next(it) if mode in ("rms", "ln_silu") else None
    p1_ref = next(it) if mode == "ln_silu" else None
    w_ref = next(it)
    wg_ref = next(it) if glu else None
    res_ref = next(it) if has_res else None
    o_ref = next(it)
    tm = x_ref.shape[0]

    if x_ref.dtype == BF16:
        a = x_ref[...]
    else:
        h_ref = next(it)

        @pl.when(pl.program_id(1) == 0)
        def _():
            def chunk(rows):
                xv = x_ref[rows, :]
                if mode == "rms":
                    xv = _rms_rows(xv, p0_ref[...])
                elif mode == "ln_silu":
                    xv = _ln_silu_rows(xv, p0_ref[...], p1_ref[...])
                h_ref[rows, :] = xv.astype(BF16)

            _row_chunks(tm, chunk)

        a = h_ref[...]
    acc = jnp.dot(a, w_ref[...], preferred_element_type=F32)
    if glu:
        acc = acc * _sigmoid(jnp.dot(a, wg_ref[...], preferred_element_type=F32))
    if has_res:
        acc = res_ref[...] + acc
    o_ref[...] = acc


def _mm_vmem_bytes(tm, k, tn, x_bytes, glu, has_res):
    n_w = 2 if glu else 1
    blocks = 2 * (tm * k * x_bytes + n_w * k * tn * 2 + (2 if has_res else 1) * tm * tn * 4)
    return blocks + (tm * k * 2 if x_bytes == 4 else 0) + (n_w + 1) * tm * tn * 4


def _pro_mm(x, w, layer, *, mode, p0=None, p1=None, glu=False, res=None, name):
    m, k = x.shape
    n = w.shape[2] // (2 if glu else 1)
    tm = _tile(m, ROWS_MM, 16)
    tn = _tile(n, COLS_MM, LANES)
    while tn > LANES and _mm_vmem_bytes(tm, k, tn, x.dtype.itemsize, glu, res is not None) > MM_VMEM_BUDGET:
        tn = _tile(n, tn // 2, LANES)
    nn = n // tn
    args = [x]
    in_specs = [pl.BlockSpec((tm, k), lambda i, j: (i, 0))]
    for p in (p0, p1):
        if p is not None:
            args.append(p.reshape(1, k))
            in_specs.append(pl.BlockSpec((1, k), lambda i, j: (0, 0)))
    args.append(w)
    in_specs.append(pl.BlockSpec((None, k, tn), lambda i, j: (layer, 0, j)))
    if glu:
        args.append(w)
        in_specs.append(pl.BlockSpec((None, k, tn), lambda i, j: (layer, 0, j + nn)))
    if res is not None:
        args.append(res)
        in_specs.append(pl.BlockSpec((tm, tn), lambda i, j: (i, j)))
    return pl.pallas_call(
        functools.partial(_pro_mm_kernel, mode=mode, glu=glu, has_res=res is not None),
        grid=(m // tm, nn),
        in_specs=in_specs,
        out_specs=pl.BlockSpec((tm, tn), lambda i, j: (i, j)),
        out_shape=jax.ShapeDtypeStruct((m, n), F32),
        scratch_shapes=[] if x.dtype == BF16 else [pltpu.VMEM((tm, k), BF16)],
        compiler_params=_params(2),
        name=name,
    )(*args)


def _conv_long_kernel(v_ref, st_ref, w_ref, b_ref, c_ref, nb_ref, xp_ref, wk_ref, cs_ref, *, tps):
    tm, cb = v_ref.shape
    width = w_ref.shape[0]
    off = HIST_PAD - (width - 1)
    rc = _tile(tm, 128, SUBLANES)
    m = pl.program_id(1)
    start = (m % tps) == 0
    cols = [slice(j * LANES, (j + 1) * LANES) for j in range(cb // LANES)]

    @pl.when(start)
    def _():
        for j, cs in enumerate(cols):
            xp_ref[j, 0:HIST_PAD, :] = st_ref[0, :, cs]

    @pl.when(jnp.logical_not(start))
    def _():
        for j in range(len(cols)):
            xp_ref[j, 0:HIST_PAD, :] = xp_ref[j, tm:tm + HIST_PAD, :]

    for j, cs in enumerate(cols):
        xp_ref[j, HIST_PAD:HIST_PAD + tm, :] = v_ref[:, cs]
        wk_ref[j, 0:width, :] = w_ref[:, cs]
        wk_ref[j, width:width + 1, :] = b_ref[:, cs]

    def column(j, carry):
        for r in range(tm // rc):
            acc = jnp.broadcast_to(wk_ref[j, width:width + 1, :], (rc, LANES))
            for rho in range(min(SUBLANES, width)):
                taps = range(rho, width, SUBLANES)
                lo = r * rc + off + rho
                win = xp_ref[j, lo:lo + rc + SUBLANES * (len(taps) - 1), :]
                for i, k in enumerate(taps):
                    acc = acc + wk_ref[j, k:k + 1, :] * win[SUBLANES * i:SUBLANES * i + rc]
            cs_ref[j, r * rc:(r + 1) * rc, :] = acc
        return carry

    lax.fori_loop(0, len(cols), column, 0)
    for j, cs in enumerate(cols):
        c_ref[:, cs] = cs_ref[j]
        nb_ref[0, :, cs] = xp_ref[j, tm:tm + HIST_PAD, :]


def _conv_long(v, st, w_dw, b_dw, *, n_seq):
    m, c = v.shape
    width = w_dw.shape[0]
    t = m // n_seq
    tm = _tile(t, ROWS_CONV, SUBLANES)
    assert tm >= HIST_PAD
    tps = t // tm
    cb = _tile(c, COLS_CONV, LANES)
    ncl = cb // LANES
    return pl.pallas_call(
        functools.partial(_conv_long_kernel, tps=tps),
        grid=(c // cb, m // tm),
        in_specs=[pl.BlockSpec((tm, cb), lambda j, i: (i, j)),
                  pl.BlockSpec((1, HIST_PAD, cb), lambda j, i: (i // tps, 0, j)),
                  pl.BlockSpec((width, cb), lambda j, i: (0, j)),
                  pl.BlockSpec((1, cb), lambda j, i: (0, j))],
        out_specs=[pl.BlockSpec((tm, cb), lambda j, i: (i, j)),
                   pl.BlockSpec((1, HIST_PAD, cb), lambda j, i: (i // tps, 0, j))],
        out_shape=[jax.ShapeDtypeStruct((m, c), F32),
                   jax.ShapeDtypeStruct((n_seq, HIST_PAD, c), F32)],
        scratch_shapes=[pltpu.VMEM((ncl, HIST_PAD + tm, LANES), F32),
                        pltpu.VMEM((ncl, width + 1, LANES), F32),
                        pltpu.VMEM((ncl, tm, LANES), F32)],
        compiler_params=_params(2),
        name="conv_long",
    )(v, st, w_dw, b_dw.reshape(1, c))


def _conv_slab_kernel(*refs, hist, n_tok, s):
    v_ref = refs[0]
    st_refs = refs[1:1 + hist]
    w_ref, b_ref, c_ref, nb_ref = refs[1 + hist:]
    width = w_ref.shape[0]

    def slab(i):
        if i < hist:
            return st_refs[i][...]
        return v_ref[(i - hist) * s:(i - hist + 1) * s, :]

    for t in range(n_tok):
        acc = jnp.broadcast_to(b_ref[...], (s, LANES))
        for k in range(width):
            acc = acc + w_ref[k:k + 1, :] * slab(t + k)
        c_ref[t * s:(t + 1) * s, :] = acc
    for j in range(hist):
        nb_ref[j] = slab(j + n_tok)


def _conv_slab(v, st2d, w_dw, b_dw, *, s):
    m, c = v.shape
    width = w_dw.shape[0]
    hist = width - 1
    n_tok = m // s
    ncb = c // LANES
    in_specs = [pl.BlockSpec((m, LANES), lambda j: (0, j))]
    in_specs += [pl.BlockSpec((s, LANES), lambda j, jj=jj: (0, jj * ncb + j)) for jj in range(hist)]
    in_specs += [pl.BlockSpec((width, LANES), lambda j: (0, j)),
                 pl.BlockSpec((1, LANES), lambda j: (0, j))]
    return pl.pallas_call(
        functools.partial(_conv_slab_kernel, hist=hist, n_tok=n_tok, s=s),
        grid=(ncb,),
        in_specs=in_specs,
        out_specs=[pl.BlockSpec((m, LANES), lambda j: (0, j)),
                   pl.BlockSpec((hist, s, LANES), lambda j: (0, 0, j))],
        out_shape=[jax.ShapeDtypeStruct((m, c), F32),
                   jax.ShapeDtypeStruct((hist, s, c), F32)],
        compiler_params=_params(1),
        name="conv_slab",
    )(v, *([st2d] * hist), w_dw, b_dw.reshape(1, c))


def _select_by_group(gid, vals):
    out = vals[-1]
    for g in range(len(vals) - 2, -1, -1):
        out = jnp.where(gid == g, vals[g], out)
    return out


def _pool_long_kernel(h_ref, st_ref, p_ref, nb_ref, xp_ref, s1_ref, s2_ref, s3_ref, s4_ref,
                      *, tps, start, lanes_per_group):
    tm, cb = h_ref.shape
    n = HIST_PAD + tm
    m = pl.program_id(1)
    first = (m % tps) == 0
    cols = [slice(j * LANES, (j + 1) * LANES) for j in range(cb // LANES)]

    @pl.when(first)
    def _():
        for j, cs in enumerate(cols):
            xp_ref[j, 0:HIST_PAD, :] = st_ref[0, :, cs]

    @pl.when(jnp.logical_not(first))
    def _():
        for j in range(len(cols)):
            xp_ref[j, 0:HIST_PAD, :] = xp_ref[j, tm:tm + HIST_PAD, :]

    for j, cs in enumerate(cols):
        gid = (pl.program_id(0) * len(cols) + j) // lanes_per_group
        xp_ref[j, HIST_PAD:n, :] = h_ref[:, cs]
        s1_ref[8:n, :] = xp_ref[j, 8:n, :] + xp_ref[j, 7:n - 1, :]
        s2_ref[16:n, :] = s1_ref[16:n, :] + s1_ref[14:n - 2, :]
        s3_ref[24:n, :] = s2_ref[24:n, :] + s2_ref[20:n - 4, :]
        s4_ref[32:n, :] = s3_ref[32:n, :] + s3_ref[24:n - 8, :]
        sums = [r[HIST_PAD:n, :] for r in (s1_ref, s2_ref, s3_ref, s4_ref)]
        if start >= max(POOL_WINDOWS) - 1:
            means = [sm * (1.0 / w) for sm, w in zip(sums, POOL_WINDOWS)]
        else:
            pos = start + (m % tps) * tm + lax.broadcasted_iota(jnp.int32, (tm, LANES), 0)
            means = [sm / jnp.minimum(w, pos + 1).astype(F32) for sm, w in zip(sums, POOL_WINDOWS)]
        p_ref[:, cs] = (_select_by_group(gid, means) - h_ref[:, cs]).astype(p_ref.dtype)
        nb_ref[0, :, cs] = xp_ref[j, tm:tm + HIST_PAD, :]


def _pool_long(h, st, *, n_seq, start):
    m, d = h.shape
    t = m // n_seq
    tm = _tile(t, ROWS_CONV, 16)
    assert tm >= HIST_PAD
    tps = t // tm
    lanes_per_group = d // len(POOL_WINDOWS) // LANES
    cb = _tile(d, COLS_CONV, LANES)
    return pl.pallas_call(
        functools.partial(_pool_long_kernel, tps=tps, start=start, lanes_per_group=lanes_per_group),
        grid=(d // cb, m // tm),
        in_specs=[pl.BlockSpec((tm, cb), lambda j, i: (i, j)),
                  pl.BlockSpec((1, HIST_PAD, cb), lambda j, i: (i // tps, 0, j))],
        out_specs=[pl.BlockSpec((tm, cb), lambda j, i: (i, j)),
                   pl.BlockSpec((1, HIST_PAD, cb), lambda j, i: (i // tps, 0, j))],
        out_shape=[jax.ShapeDtypeStruct((m, d), BF16),
                   jax.ShapeDtypeStruct((n_seq, HIST_PAD, d), F32)],
        scratch_shapes=[pltpu.VMEM((cb // LANES, HIST_PAD + tm, LANES), F32)]
        + [pltpu.VMEM((HIST_PAD + tm, LANES), F32) for _ in range(4)],
        compiler_params=_params(2),
        name="pool_long",
    )(h, st)


def _pool_slab_kernel(*refs, hist, n_tok, s, start, lanes_per_group):
    h_ref = refs[0]
    st_refs = refs[1:1 + hist]
    p_ref, nb_ref = refs[1 + hist:]
    gid = pl.program_id(0) // lanes_per_group

    def slab(i):
        if i < hist:
            return st_refs[i][...]
        return h_ref[(i - hist) * s:(i - hist + 1) * s, :]

    n = hist + n_tok
    level = [slab(i) for i in range(n)]
    levels = []
    step = 1
    for _ in POOL_WINDOWS:
        level = [level[i] + level[i - step] if i >= 2 * step - 1 else None for i in range(n)]
        levels.append(level)
        step *= 2
    for t in range(n_tok):
        means = [lv[hist + t] * (1.0 / min(w, start + t + 1)) for lv, w in zip(levels, POOL_WINDOWS)]
        p_ref[t * s:(t + 1) * s, :] = (_select_by_group(gid, means) - slab(hist + t)).astype(p_ref.dtype)
    for j in range(hist):
        nb_ref[j] = slab(j + n_tok)


def _pool_slab(h, st2d, *, s, start):
    m, d = h.shape
    hist = max(POOL_WINDOWS) - 1
    n_tok = m // s
    ncb = d // LANES
    lanes_per_group = d // len(POOL_WINDOWS) // LANES
    in_specs = [pl.BlockSpec((m, LANES), lambda j: (0, j))]
    in_specs += [pl.BlockSpec((s, LANES), lambda j, jj=jj: (0, jj * ncb + j)) for jj in range(hist)]
    return pl.pallas_call(
        functools.partial(_pool_slab_kernel, hist=hist, n_tok=n_tok, s=s, start=start,
                          lanes_per_group=lanes_per_group),
        grid=(ncb,),
        in_specs=in_specs,
        out_specs=[pl.BlockSpec((m, LANES), lambda j: (0, j)),
                   pl.BlockSpec((hist, s, LANES), lambda j: (0, 0, j))],
        out_shape=[jax.ShapeDtypeStruct((m, d), BF16),
                   jax.ShapeDtypeStruct((hist, s, d), F32)],
        compiler_params=_params(1),
        name="pool_slab",
    )(h, *([st2d] * hist))


def _grp_mm_kernel(p_ref, w_ref, sc_ref, res_ref, o_ref):
    y = jnp.dot(p_ref[...], w_ref[0], preferred_element_type=F32)
    o_ref[...] = res_ref[...] + y * sc_ref[...]


def _grp_mm(p, w_grp, layer, scale, res):
    m, d = p.shape
    _, ng, gc, _ = w_grp.shape
    tm = _tile(m, ROWS_MM, 16)
    return pl.pallas_call(
        _grp_mm_kernel,
        grid=(m // tm, ng),
        in_specs=[pl.BlockSpec((tm, gc), lambda i, g: (i, g)),
                  pl.BlockSpec((None, 1, gc, gc), lambda i, g: (layer, g, 0, 0)),
                  pl.BlockSpec((1, gc), lambda i, g: (0, g)),
                  pl.BlockSpec((tm, gc), lambda i, g: (i, g))],
        out_specs=pl.BlockSpec((tm, gc), lambda i, g: (i, g)),
        out_shape=jax.ShapeDtypeStruct((m, d), F32),
        compiler_params=_params(2),
        name="grp_mm",
    )(p, w_grp, scale.reshape(1, d), res)


def _cumsum_rows(x):
    n = x.shape[0]
    row = lax.broadcasted_iota(jnp.int32, x.shape, 0)
    d = 1
    while d < n:
        x = x + jnp.where(row >= d, pltpu.roll(x, d, axis=0), 0.0)
        d *= 2
    return x


def _level_table(c):
    t = np.arange(c)[:, None]
    s = np.arange(c)[None, :]
    x = np.maximum(t ^ s, 1)
    top = np.left_shift(1, np.floor(np.log2(x)).astype(np.int64))
    return np.where(s < t, top, np.where(s == t, 0, -1)).astype(np.int32)


def _level_ref(g_cum, length):
    c = g_cum.shape[0]
    if length >= SUBLANES:
        parts = [jnp.broadcast_to(g_cum[b + length - 1:b + length, :], (2 * length, HGRN_DK))
                 for b in range(0, c, 2 * length)]
        return parts[0] if len(parts) == 1 else jnp.concatenate(parts, axis=0)
    g3 = g_cum.reshape(c // SUBLANES, SUBLANES, HGRN_DK)
    sub = lax.broadcasted_iota(jnp.int32, g3.shape, 1)
    ref = None
    for b in range(SUBLANES - 2 * length, -1, -2 * length):
        cand = jnp.broadcast_to(g3[:, b + length - 1:b + length, :], g3.shape)
        ref = cand if ref is None else jnp.where(sub < b + 2 * length, cand, ref)
    return ref.reshape(c, HGRN_DK)


def _tile_pairs(q, kk, g_cum, inp):
    c = q.shape[0]
    t_idx = lax.broadcasted_iota(jnp.int32, (c, HGRN_DK), 0)
    ps = []
    for s in range(c):
        d = jnp.where(t_idx >= s, g_cum - g_cum[s:s + 1, :], -jnp.inf)
        ps.append(q * jnp.exp(d) * kk[s:s + 1, :])
    a_all = jnp.dot(jnp.concatenate(ps, axis=0), jnp.ones((HGRN_DK, LANES), F32),
                    preferred_element_type=F32)
    out = jnp.zeros((c, HGRN_DK), F32)
    for s in range(c):
        out = out + a_all[s * c:(s + 1) * c] * inp[s:s + 1, :]
    return out


def _hgrn_chunk(q_raw, fz, inp, lb, s0, lv):
    c = q_raw.shape[0]
    f = lb + (1.0 - lb) * _sigmoid(fz)
    kk = 1.0 - f
    q = _silu(q_raw)
    g_cum = _cumsum_rows(jnp.log(f))
    o = jnp.dot(q * jnp.exp(g_cum), s0, preferred_element_type=F32)
    if c == SUBLANES:
        o = o + _tile_pairs(q, kk, g_cum, inp)
    else:
        scores = jnp.zeros((c, c), F32)
        length = 0
        while length < c:
            if length == 0:
                qt, kt = q, kk
            else:
                e = jnp.exp(-jnp.abs(g_cum - _level_ref(g_cum, length)))
                qt, kt = q * e, kk * e
            sc = lax.dot_general(qt, kt, (((1,), (1,)), ((), ())), preferred_element_type=F32)
            scores = jnp.where(lv == length, sc, scores)
            length = max(1, 2 * length)
        o = o + jnp.dot(scores, inp, preferred_element_type=F32)
    g_end = g_cum[c - 1:c, :]
    kt = kk * jnp.exp(g_end - g_cum)
    upd = lax.dot_general(kt, inp, (((0,), (0,)), ((), ())), preferred_element_type=F32)
    decay = jnp.transpose(jnp.broadcast_to(jnp.exp(g_end), (HGRN_DK, HGRN_DK)))
    return o, decay * s0 + upd


def _hgrn_kernel(q_ref, f_ref, i_ref, g_ref, clb_ref, gn_ref, lv_ref, s0_ref, o_ref, sf_ref,
                 *, layer, hu, chunk):
    @pl.when(pl.program_id(2) == 0)
    def _():
        sf_ref[...] = s0_ref[...]

    clb = clb_ref[...]
    e = jnp.exp(clb - jnp.max(clb, axis=0, keepdims=True))
    sm = e / jnp.sum(e, axis=0, keepdims=True)
    lb_all = jnp.zeros_like(sm[0:1])
    for i in range(1, layer + 1):
        lb_all = lb_all + sm[i:i + 1]
    sb = sf_ref.shape[0]
    tb = q_ref.shape[0] // sb
    lv = lv_ref[...]
    for si in range(sb):
        for h in range(hu):
            hs = slice(h * HGRN_DK, (h + 1) * HGRN_DK)
            state = sf_ref[si, h]
            for cix in range(tb // chunk):
                rows = slice(si * tb + cix * chunk, si * tb + (cix + 1) * chunk)
                o, state = _hgrn_chunk(q_ref[rows, hs], f_ref[rows, hs], i_ref[rows, hs],
                                       lb_all[:, hs], state, lv)
                o = o * lax.rsqrt(jnp.mean(o * o, axis=-1, keepdims=True) + NORM_EPS)
                o_ref[rows, hs] = o * gn_ref[:, hs] * _silu(g_ref[rows, hs])
            sf_ref[si, h] = state


def _hgrn(proj, c_lb, g_norm, s0, *, n_seq, layer):
    m, d4 = proj.shape
    d = d4 // 4
    nh = d // HGRN_DK
    t = m // n_seq
    chunk = _tile(t, 128, SUBLANES)
    tb = _tile(t, 2 * chunk, chunk)
    short = t == tb == chunk == SUBLANES
    hu = nh if short else min(2, nh)
    sb = _tile(n_seq, 4, 1) if short else 1
    nhg = nh // hu
    ntb = t // tb
    wblk = hu * HGRN_DK
    rblk = sb * tb
    col = lambda part: (lambda b, hg, c: (b * ntb + c, part * nhg + hg))
    return pl.pallas_call(
        functools.partial(_hgrn_kernel, layer=layer, hu=hu, chunk=chunk),
        grid=(n_seq // sb, nhg, ntb),
        in_specs=[pl.BlockSpec((rblk, wblk), col(0)),
                  pl.BlockSpec((rblk, wblk), col(1)),
                  pl.BlockSpec((rblk, wblk), col(2)),
                  pl.BlockSpec((rblk, wblk), col(3)),
                  pl.BlockSpec((c_lb.shape[0], wblk), lambda b, hg, c: (0, hg)),
                  pl.BlockSpec((1, wblk), lambda b, hg, c: (0, hg)),
                  pl.BlockSpec((chunk, chunk), lambda b, hg, c: (0, 0)),
                  pl.BlockSpec((sb, hu, HGRN_DK, HGRN_DK), lambda b, hg, c: (b, hg, 0, 0))],
        out_specs=[pl.BlockSpec((rblk, wblk), lambda b, hg, c: (b * ntb + c, hg)),
                   pl.BlockSpec((sb, hu, HGRN_DK, HGRN_DK), lambda b, hg, c: (b, hg, 0, 0))],
        out_shape=[jax.ShapeDtypeStruct((m, d), F32),
                   jax.ShapeDtypeStruct(s0.shape, F32)],
        compiler_params=_params(3),
        name="hgrn",
    )(proj, proj, proj, proj, c_lb, g_norm.reshape(1, d), jnp.asarray(_level_table(chunk)), s0)


def _ffn_norm(x_ref, g_ref, h_ref):
    def chunk(rows):
        h_ref[rows, :] = _rms_rows(x_ref[rows, :], g_ref[...]).astype(BF16)

    _row_chunks(x_ref.shape[0], chunk)


def _ffn_up_long_kernel(x_ref, g_ref, wa_ref, wb_ref, dwa_ref, dwb_ref, ba_ref, bb_ref,
                        sta_ref, stb_ref, act_ref, nba_ref, nbb_ref,
                        h_ref, xpa_ref, xpb_ref, cra_ref, crb_ref, twa_ref, twb_ref, *, tps):
    tm = x_ref.shape[0]
    pad = FFN_HIST_PAD
    m = pl.program_id(0)
    f = pl.program_id(1)
    first = (m % tps) == 0

    @pl.when(f == 0)
    def _():
        _ffn_norm(x_ref, g_ref, h_ref)

    tf = act_ref.shape[1]
    cols = [slice(j * LANES, (j + 1) * LANES) for j in range(tf // LANES)]

    @pl.when(first)
    def _():
        for j, cs in enumerate(cols):
            xpa_ref[j, 0:pad, :] = sta_ref[0, :, cs]
            xpb_ref[j, 0:pad, :] = stb_ref[0, :, cs]

    @pl.when(jnp.logical_not(first))
    def _():
        for j, cs in enumerate(cols):
            xpa_ref[j, 0:pad, :] = cra_ref[f, :, cs]
            xpb_ref[j, 0:pad, :] = crb_ref[f, :, cs]

    rb = _tile(tm, max(tm // 4, 16), 16)
    rc = _tile(rb, 64, 16)

    def project(i):
        rows = slice(i * rb, (i + 1) * rb)
        for xp_ref, w_ref in ((xpa_ref, wa_ref), (xpb_ref, wb_ref)):
            u = jnp.dot(h_ref[rows, :], w_ref[...], preferred_element_type=F32)
            for j, cs in enumerate(cols):
                xp_ref[j, pad + i * rb:pad + (i + 1) * rb, :] = u[:, cs]

    for tw_ref, dw_ref, b_ref in ((twa_ref, dwa_ref, ba_ref), (twb_ref, dwb_ref, bb_ref)):
        for k in range(3):
            tw_ref[k] = jnp.broadcast_to(dw_ref[k:k + 1, :], (pad, tf))
        tw_ref[3] = jnp.broadcast_to(b_ref[...], (pad, tf))

    def conv(xp_ref, tw_ref, j, r0):
        def rows_back(k):
            return xp_ref[j, r0 + pad - k:r0 + pad - k + rc, :].reshape(rc // pad, pad, LANES)

        tap = lambda k: tw_ref[k, :, cols[j]]
        return tap(0) * rows_back(2) + tap(1) * rows_back(1) + tap(2) * rows_back(0) + tap(3)

    def gate(i):
        for r0 in range(i * rb, (i + 1) * rb, rc):
            for j, cs in enumerate(cols):
                ca = conv(xpa_ref, twa_ref, j, r0)
                cb = conv(xpb_ref, twb_ref, j, r0)
                act_ref[r0:r0 + rc, cs] = (_silu(ca) * cb).reshape(rc, LANES).astype(BF16)

    project(0)
    for i in range(1, tm // rb):
        project(i)
        gate(i - 1)
    gate(tm // rb - 1)
    for xp_ref, cr_ref, nb_ref in ((xpa_ref, cra_ref, nba_ref), (xpb_ref, crb_ref, nbb_ref)):
        for j, cs in enumerate(cols):
            tail = xp_ref[j, tm:tm + pad, :]
            cr_ref[f, :, cs] = tail
            nb_ref[0, :, cs] = tail


def _ffn_up_long(x, g, w_up, layer, w_dw, b_dw, st, *, n_seq):
    m, d = x.shape
    f2 = w_up.shape[2]
    ff = f2 // 2
    t = m // n_seq
    tm = _tile(t, ROWS_FFN, 16)
    tps = t // tm
    tf = _tile(ff, COLS_FFN, LANES)
    nf = ff // tf
    pad = FFN_HIST_PAD
    lo = lambda i, j: (0, j)
    hi = lambda i, j: (0, j + nf)
    act, nba, nbb = pl.pallas_call(
        functools.partial(_ffn_up_long_kernel, tps=tps),
        grid=(m // tm, nf),
        in_specs=[pl.BlockSpec((tm, d), lambda i, j: (i, 0)),
                  pl.BlockSpec((1, d), lambda i, j: (0, 0)),
                  pl.BlockSpec((None, d, tf), lambda i, j: (layer, 0, j)),
                  pl.BlockSpec((None, d, tf), lambda i, j: (layer, 0, j + nf)),
                  pl.BlockSpec((3, tf), lo), pl.BlockSpec((3, tf), hi),
                  pl.BlockSpec((1, tf), lo), pl.BlockSpec((1, tf), hi),
                  pl.BlockSpec((1, pad, tf), lambda i, j: (i // tps, 0, j)),
                  pl.BlockSpec((1, pad, tf), lambda i, j: (i // tps, 0, j + nf))],
        out_specs=[pl.BlockSpec((tm, tf), lambda i, j: (i, j)),
                   pl.BlockSpec((1, pad, tf), lambda i, j: (i, 0, j)),
                   pl.BlockSpec((1, pad, tf), lambda i, j: (i, 0, j))],
        out_shape=[jax.ShapeDtypeStruct((m, ff), BF16),
                   jax.ShapeDtypeStruct((m // tm, pad, ff), F32),
                   jax.ShapeDtypeStruct((m // tm, pad, ff), F32)],
        scratch_shapes=[pltpu.VMEM((tm, d), BF16),
                        pltpu.VMEM((tf // LANES, pad + tm, LANES), F32),
                        pltpu.VMEM((tf // LANES, pad + tm, LANES), F32),
                        pltpu.VMEM((nf, pad, tf), F32), pltpu.VMEM((nf, pad, tf), F32),
                        pltpu.VMEM((4, pad, tf), F32), pltpu.VMEM((4, pad, tf), F32)],
        compiler_params=_params(2),
        name="ffn_up_long",
    )(x, g.reshape(1, d), w_up, w_up, w_dw, w_dw, b_dw.reshape(1, f2), b_dw.reshape(1, f2), st, st)
    last = slice(tps - 1, None, tps)
    return act, jnp.concatenate([nba[last, pad - 2:], nbb[last, pad - 2:]], axis=-1)


def _ffn_up_slab_kernel(x_ref, g_ref, wa_ref, wb_ref, dwa_ref, dwb_ref, ba_ref, bb_ref,
                        sa0_ref, sa1_ref, sb0_ref, sb1_ref,
                        act_ref, na0_ref, na1_ref, nb0_ref, nb1_ref,
                        h_ref, xpa_ref, xpb_ref, *, s):
    m = x_ref.shape[0]

    @pl.when(pl.program_id(0) == 0)
    def _():
        _ffn_norm(x_ref, g_ref, h_ref)

    def half(w_ref, dw_ref, b_ref, s0_ref, s1_ref, n0_ref, n1_ref, xp_ref):
        u = jnp.dot(h_ref[...], w_ref[...], preferred_element_type=F32)
        xp_ref[0:s, :] = s0_ref[...]
        xp_ref[s:2 * s, :] = s1_ref[...]
        xp_ref[2 * s:2 * s + m, :] = u
        n0_ref[...] = xp_ref[m:m + s, :]
        n1_ref[...] = xp_ref[m + s:m + 2 * s, :]
        return (dw_ref[0:1, :] * xp_ref[0:m, :] + dw_ref[1:2, :] * xp_ref[s:s + m, :]
                + dw_ref[2:3, :] * u + b_ref[...])

    ca = half(wa_ref, dwa_ref, ba_ref, sa0_ref, sa1_ref, na0_ref, na1_ref, xpa_ref)
    cb = half(wb_ref, dwb_ref, bb_ref, sb0_ref, sb1_ref, nb0_ref, nb1_ref, xpb_ref)
    act_ref[...] = (_silu(ca) * cb).astype(BF16)


def _ffn_up_slab(x, g, w_up, layer, w_dw, b_dw, st2d, *, s):
    m, d = x.shape
    f2 = w_up.shape[2]
    ff = f2 // 2
    tf = _tile(ff, COLS_FFN, LANES)
    nf = ff // tf
    lo = lambda j: (0, j)
    hi = lambda j: (0, j + nf)
    outs = pl.pallas_call(
        functools.partial(_ffn_up_slab_kernel, s=s),
        grid=(nf,),
        in_specs=[pl.BlockSpec((m, d), lambda j: (0, 0)),
                  pl.BlockSpec((1, d), lambda j: (0, 0)),
                  pl.BlockSpec((None, d, tf), lambda j: (layer, 0, j)),
                  pl.BlockSpec((None, d, tf), lambda j: (layer, 0, j + nf)),
                  pl.BlockSpec((3, tf), lo), pl.BlockSpec((3, tf), hi),
                  pl.BlockSpec((1, tf), lo), pl.BlockSpec((1, tf), hi),
                  pl.BlockSpec((s, tf), lo), pl.BlockSpec((s, tf), lambda j: (0, j + 2 * nf)),
                  pl.BlockSpec((s, tf), hi), pl.BlockSpec((s, tf), lambda j: (0, j + 3 * nf))],
        out_specs=[pl.BlockSpec((m, tf), lo)] + [pl.BlockSpec((s, tf), lo)] * 4,
        out_shape=[jax.ShapeDtypeStruct((m, ff), BF16)] + [jax.ShapeDtypeStruct((s, ff), F32)] * 4,
        scratch_shapes=[pltpu.VMEM((m, d), BF16),
                        pltpu.VMEM((2 * s + m, tf), F32), pltpu.VMEM((2 * s + m, tf), F32)],
        compiler_params=_params(1),
        name="ffn_up_slab",
    )(x, g.reshape(1, d), w_up, w_up, w_dw, w_dw, b_dw.reshape(1, f2), b_dw.reshape(1, f2),
      st2d, st2d, st2d, st2d)
    act, na0, na1, nb0, nb1 = outs
    new = jnp.stack([jnp.concatenate([na0, nb0], axis=-1), jnp.concatenate([na1, nb1], axis=-1)], axis=1)
    return act, new


def _front_pad(st, pad):
    return jnp.pad(st, ((0, 0), (pad - st.shape[1], 0), (0, 0)))


def _trunk(x, st_a, st_b, st_c, st_f, start, wts, *, slab):
    (norm_mix, norm_ffn, norm_final, a_w_pw1, a_w_dw, a_b_dw, a_ln_g, a_ln_b, a_w_pw2,
     b_w_grp, b_scale, c_lb, c_w_in, c_g_norm, c_w_o, f_w_up, f_w_dw, f_b_dw, f_w_down) = wts
    bsz, t, d = x.shape
    depth = norm_mix.shape[0]
    m = bsz * t
    if slab:
        to_rows = lambda a: a.transpose(1, 0, 2).reshape(m, a.shape[-1])
        from_rows = lambda a: a.reshape(t, bsz, a.shape[-1]).transpose(1, 0, 2)
    else:
        to_rows = lambda a: a.reshape(m, a.shape[-1])
        from_rows = lambda a: a.reshape(bsz, t, a.shape[-1])
    xr = to_rows(x)
    new_a, new_b, new_c, new_f = [], [], [], []
    for layer in range(depth):
        kind, j = layer % 3, layer // 3
        if kind == 0:
            v = _pro_mm(xr, a_w_pw1, j, mode="rms", p0=norm_mix[layer], glu=True, name="a_pw1")
            hist = a_w_dw.shape[1] - 1
            if slab:
                c_pre, nb = _conv_slab(v, st_a[j].reshape(bsz, -1), a_w_dw[j], a_b_dw[j], s=bsz)
                nb = nb.transpose(1, 0, 2)
            else:
                c_pre, nb = _conv_long(v, _front_pad(st_a[j], HIST_PAD), a_w_dw[j], a_b_dw[j], n_seq=bsz)
                nb = nb[:, HIST_PAD - hist:]
            new_a.append(nb)
            xr = _pro_mm(c_pre, a_w_pw2, j, mode="ln_silu", p0=a_ln_g[j], p1=a_ln_b[j], res=xr, name="a_pw2")
        elif kind == 1:
            h = _rms(xr, norm_mix[layer])
            hist = max(POOL_WINDOWS) - 1
            if slab:
                pooled, nb = _pool_slab(h, st_b[j].reshape(bsz, -1), s=bsz, start=start)
                nb = nb.transpose(1, 0, 2)
            else:
                pooled, nb = _pool_long(h, _front_pad(st_b[j], HIST_PAD), n_seq=bsz, start=start)
                nb = nb[:, HIST_PAD - hist:]
            new_b.append(nb)
            xr = _grp_mm(pooled, b_w_grp, j, b_scale[j], xr)
        else:
            proj = _pro_mm(xr, c_w_in, j, mode="rms", p0=norm_mix[layer], name="c_in")
            if slab:
                proj = from_rows(proj).reshape(m, proj.shape[-1])
            og, s_fin = _hgrn(proj, c_lb, c_g_norm[j], st_c[j], n_seq=bsz, layer=layer)
            if slab:
                og = to_rows(og.reshape(bsz, t, d))
            new_c.append(s_fin)
            xr = _pro_mm(og, c_w_o, j, mode="cast", res=xr, name="c_out")
        if slab:
            act, nb = _ffn_up_slab(xr, norm_ffn[layer], f_w_up, layer, f_w_dw[layer], f_b_dw[layer],
                                   st_f[layer].reshape(bsz, -1), s=bsz)
        else:
            act, nb = _ffn_up_long(xr, norm_ffn[layer], f_w_up, layer, f_w_dw[layer], f_b_dw[layer],
                                   _front_pad(st_f[layer], FFN_HIST_PAD), n_seq=bsz)
        new_f.append(nb)
        xr = _pro_mm(act, f_w_down, layer, mode="cast", res=xr, name="f_down")
    y = from_rows(_rms(xr, norm_final))
    return y, jnp.stack(new_a), jnp.stack(new_b), jnp.stack(new_c), jnp.stack(new_f)


def kernel(x_prompt, x_sample, state_conv_a, state_pool, state_hgrn, state_ffn_conv, norm_mix, norm_ffn, norm_final, a_w_pw1, a_w_dw, a_b_dw, a_ln_g, a_ln_b, a_w_pw2, b_w_grp, b_scale, c_lb, c_w_in, c_g_norm, c_w_o, f_w_up, f_w_dw, f_b_dw, f_w_down):
    bf = lambda w: w.astype(BF16)
    wts = (norm_mix, norm_ffn, norm_final, bf(a_w_pw1), a_w_dw, a_b_dw, a_ln_g, a_ln_b, bf(a_w_pw2),
           bf(b_w_grp), b_scale, c_lb, bf(c_w_in), c_g_norm, bf(c_w_o), bf(f_w_up), f_w_dw, f_b_dw,
           bf(f_w_down))
    bp = x_prompt.shape[0]
    zeros = lambda st: jnp.zeros((st.shape[0], bp) + st.shape[2:], st.dtype)
    yp, pa, pb, pc, pf = _trunk(x_prompt, zeros(state_conv_a), zeros(state_pool), zeros(state_hgrn),
                                zeros(state_ffn_conv), 0, wts, slab=False)
    ys, sa, sb, sc, sf = _trunk(x_sample, state_conv_a, state_pool, state_hgrn, state_ffn_conv,
                                PAST_LEN, wts, slab=True)
    return (yp, ys, pa, sa, pb, sb, pc, sc, pf, sf)
```

```python
import functools

import numpy as np
import jax
import jax.numpy as jnp
from jax import lax
from jax.experimental import pallas as pl
from jax.experimental.pallas import tpu as pltpu

F32 = jnp.float32
BF16 = jnp.bfloat16

NORM_EPS = 1e-6
POOL_WINDOWS = (2, 4, 8, 16)
HGRN_DK = 128
LANES = 128
SUBLANES = 8
HIST_PAD = 32
FFN_HIST_PAD = 8
VMEM_LIMIT_BYTES = 56 * 1024 * 1024
PAST_LEN = 16384
ROWS_MM = 1024
COLS_MM = 1024
COLS_FFN = 512
MM_VMEM_BUDGET = 46 * 1024 * 1024
ROWS_CONV = 512
COLS_CONV = 512
ROWS_FFN = 1024


def _params(n_axes):
    return pltpu.CompilerParams(dimension_semantics=("arbitrary",) * n_axes,
                                vmem_limit_bytes=VMEM_LIMIT_BYTES)


def _tile(n, pref, mult):
    t = min(pref, n)
    t -= t % mult
    while t >= mult:
        if n % t == 0:
            return t
        t -= mult
    return n


def _sigmoid(x):
    return 1.0 / (1.0 + jnp.exp(-x))


def _silu(x):
    return x * _sigmoid(x)


def _rms_rows(x, g):
    ms = jnp.mean(x * x, axis=-1, keepdims=True)
    return x * lax.rsqrt(ms + NORM_EPS) * g


def _ln_silu_rows(x, g, b):
    mu = jnp.mean(x, axis=-1, keepdims=True)
    d = x - mu
    var = jnp.mean(d * d, axis=-1, keepdims=True)
    return _silu(d * lax.rsqrt(var + NORM_EPS) * g + b)


def _row_chunks(tm, fn):
    rc = _tile(tm, 128, SUBLANES)

    def body(i, carry):
        fn(pl.ds(pl.multiple_of(i * rc, rc), rc))
        return carry

    lax.fori_loop(0, tm // rc, body, 0)


def _rms_kernel(x_ref, g_ref, o_ref):
    tm = x_ref.shape[0]

    def chunk(rows):
        o_ref[rows, :] = _rms_rows(x_ref[rows, :], g_ref[...])

    _row_chunks(tm, chunk)


def _rms(x, g):
    m, d = x.shape
    tm = _tile(m, 512, SUBLANES)
    return pl.pallas_call(
        _rms_kernel,
        grid=(m // tm,),
        in_specs=[pl.BlockSpec((tm, d), lambda i: (i, 0)),
                  pl.BlockSpec((1, d), lambda i: (0, 0))],
        out_specs=pl.BlockSpec((tm, d), lambda i: (i, 0)),
        out_shape=jax.ShapeDtypeStruct((m, d), F32),
        compiler_params=_params(1),
        name="rms",
    )(x, g.reshape(1, d))


def _pro_mm_kernel(*refs, mode, glu, has_res):
    it = iter(refs)
    x_ref = next(it)
    p0_ref = next(it) if mode in ("rms", "ln_silu") else None
    p1_ref = next(it) if mode == "ln_silu" else None
    w_ref = next(it)
    wg_ref = next(it) if glu else None
    res_ref = next(it) if has_res else None
    o_ref = next(it)
    tm = x_ref.shape[0]

    a_ref = x_ref if x_ref.dtype == BF16 else next(it)
    rb = tm if a_ref is x_ref else _tile(tm, max(tm // 4, 16), 16)

    def prologue(rows0):
        step = _tile(rb, 32, 16)
        for r0 in range(rows0, rows0 + rb, step):
            xv = x_ref[r0:r0 + step, :]
            if mode == "rms":
                xv = _rms_rows(xv, p0_ref[...])
            elif mode == "ln_silu":
                xv = _ln_silu_rows(xv, p0_ref[...], p1_ref[...])
            a_ref[r0:r0 + step, :] = xv.astype(BF16)

    def product(rows0):
        rows = slice(rows0, rows0 + rb)
        a = a_ref[rows, :]
        acc = jnp.dot(a, w_ref[...], preferred_element_type=F32)
        if glu:
            acc = acc * _sigmoid(jnp.dot(a, wg_ref[...], preferred_element_type=F32))
        if has_res:
            acc = res_ref[rows, :] + acc
        o_ref[rows, :] = acc

    if a_ref is x_ref:
        for rows0 in range(0, tm, rb):
            product(rows0)
    else:
        @pl.when(pl.program_id(1) == 0)
        def _():
            for rows0 in range(0, tm, rb):
                prologue(rows0)
                product(rows0)

        @pl.when(pl.program_id(1) > 0)
        def _():
            for rows0 in range(0, tm, rb):
                product(rows0)


def _mm_vmem_bytes(tm, k, tn, x_bytes, glu, has_res):
    n_w = 2 if glu else 1
    blocks = 2 * (tm * k * x_bytes + n_w * k * tn * 2 + (2 if has_res else 1) * tm * tn * 4)
    return blocks + (tm * k * 2 if x_bytes == 4 else 0) + (n_w + 1) * tm * tn * 4


def _pro_mm(x, w, layer, *, mode, p0=None, p1=None, glu=False, res=None, name):
    m, k = x.shape
    n = w.shape[2] // (2 if glu else 1)
    tm = _tile(m, ROWS_MM, 16)
    tn = _tile(n, COLS_MM, LANES)
    while tn > LANES and _mm_vmem_bytes(tm, k, tn, x.dtype.itemsize, glu, res is not None) > MM_VMEM_BUDGET:
        tn = _tile(n, tn // 2, LANES)
    nn = n // tn
    args = [x]
    in_specs = [pl.BlockSpec((tm, k), lambda i, j: (i, 0))]
    for p in (p0, p1):
        if p is not None:
            args.append(p.reshape(1, k))
            in_specs.append(pl.BlockSpec((1, k), lambda i, j: (0, 0)))
    args.append(w)
    in_specs.append(pl.BlockSpec((None, k, tn), lambda i, j: (layer, 0, j)))
    if glu:
        args.append(w)
        in_specs.append(pl.BlockSpec((None, k, tn), lambda i, j: (layer, 0, j + nn)))
    if res is not None:
        args.append(res)
        in_specs.append(pl.BlockSpec((tm, tn), lambda i, j: (i, j)))
    return pl.pallas_call(
        functools.partial(_pro_mm_kernel, mode=mode, glu=glu, has_res=res is not None),
        grid=(m // tm, nn),
        in_specs=in_specs,
        out_specs=pl.BlockSpec((tm, tn), lambda i, j: (i, j)),
        out_shape=jax.ShapeDtypeStruct((m, n), F32),
        scratch_shapes=[] if x.dtype == BF16 else [pltpu.VMEM((tm, k), BF16)],
        compiler_params=_params(2),
        name=name,
    )(*args)


def _conv_long_kernel(v_ref, st_ref, w_ref, b_ref, c_ref, nb_ref, xp_ref, wk_ref, cs_ref, *, tps):
    tm, cb = v_ref.shape
    width = w_ref.shape[0]
    off = HIST_PAD - (width - 1)
    rc = _tile(tm, 128, SUBLANES)
    m = pl.program_id(1)
    start = (m % tps) == 0
    cols = [slice(j * LANES, (j + 1) * LANES) for j in range(cb // LANES)]

    @pl.when(start)
    def _():
        for j, cs in enumerate(cols):
            xp_ref[j, 0:HIST_PAD, :] = st_ref[0, :, cs]

    @pl.when(jnp.logical_not(start))
    def _():
        for j in range(len(cols)):
            xp_ref[j, 0:HIST_PAD, :] = xp_ref[j, tm:tm + HIST_PAD, :]

    for j, cs in enumerate(cols):
        xp_ref[j, HIST_PAD:HIST_PAD + tm, :] = v_ref[:, cs]
        wk_ref[j, 0:width, :] = w_ref[:, cs]
        wk_ref[j, width:width + 1, :] = b_ref[:, cs]

    def column(j, carry):
        for r in range(tm // rc):
            acc = jnp.broadcast_to(wk_ref[j, width:width + 1, :], (rc, LANES))
            for rho in range(min(SUBLANES, width)):
                taps = range(rho, width, SUBLANES)
                lo = r * rc + off + rho
                win = xp_ref[j, lo:lo + rc + SUBLANES * (len(taps) - 1), :]
                for i, k in enumerate(taps):
                    acc = acc + wk_ref[j, k:k + 1, :] * win[SUBLANES * i:SUBLANES * i + rc]
            cs_ref[j, r * rc:(r + 1) * rc, :] = acc
        return carry

    lax.fori_loop(0, len(cols), column, 0)
    for j, cs in enumerate(cols):
        c_ref[:, cs] = cs_ref[j]
        nb_ref[0, :, cs] = xp_ref[j, tm:tm + HIST_PAD, :]


def _conv_long(v, st, w_dw, b_dw, *, n_seq):
    m, c = v.shape
    width = w_dw.shape[0]
    t = m // n_seq
    tm = _tile(t, ROWS_CONV, SUBLANES)
    assert tm >= HIST_PAD
    tps = t // tm
    cb = _tile(c, COLS_CONV, LANES)
    ncl = cb // LANES
    return pl.pallas_call(
        functools.partial(_conv_long_kernel, tps=tps),
        grid=(c // cb, m // tm),
        in_specs=[pl.BlockSpec((tm, cb), lambda j, i: (i, j)),
                  pl.BlockSpec((1, HIST_PAD, cb), lambda j, i: (i // tps, 0, j)),
                  pl.BlockSpec((width, cb), lambda j, i: (0, j)),
                  pl.BlockSpec((1, cb), lambda j, i: (0, j))],
        out_specs=[pl.BlockSpec((tm, cb), lambda j, i: (i, j)),
                   pl.BlockSpec((1, HIST_PAD, cb), lambda j, i: (i // tps, 0, j))],
        out_shape=[jax.ShapeDtypeStruct((m, c), F32),
                   jax.ShapeDtypeStruct((n_seq, HIST_PAD, c), F32)],
        scratch_shapes=[pltpu.VMEM((ncl, HIST_PAD + tm, LANES), F32),
                        pltpu.VMEM((ncl, width + 1, LANES), F32),
                        pltpu.VMEM((ncl, tm, LANES), F32)],
        compiler_params=_params(2),
        name="conv_long",
    )(v, st, w_dw, b_dw.reshape(1, c))


def _conv_slab_kernel(*refs, hist, n_tok, s):
    v_ref = refs[0]
    st_refs = refs[1:1 + hist]
    w_ref, b_ref, c_ref, nb_ref = refs[1 + hist:]
    width = w_ref.shape[0]

    def slab(i):
        if i < hist:
            return st_refs[i][...]
        return v_ref[(i - hist) * s:(i - hist + 1) * s, :]

    for t in range(n_tok):
        acc = jnp.broadcast_to(b_ref[...], (s, LANES))
        for k in range(width):
            acc = acc + w_ref[k:k + 1, :] * slab(t + k)
        c_ref[t * s:(t + 1) * s, :] = acc
    for j in range(hist):
        nb_ref[j] = slab(j + n_tok)


def _conv_slab(v, st2d, w_dw, b_dw, *, s):
    m, c = v.shape
    width = w_dw.shape[0]
    hist = width - 1
    n_tok = m // s
    ncb = c // LANES
    in_specs = [pl.BlockSpec((m, LANES), lambda j: (0, j))]
    in_specs += [pl.BlockSpec((s, LANES), lambda j, jj=jj: (0, jj * ncb + j)) for jj in range(hist)]
    in_specs += [pl.BlockSpec((width, LANES), lambda j: (0, j)),
                 pl.BlockSpec((1, LANES), lambda j: (0, j))]
    return pl.pallas_call(
        functools.partial(_conv_slab_kernel, hist=hist, n_tok=n_tok, s=s),
        grid=(ncb,),
        in_specs=in_specs,
        out_specs=[pl.BlockSpec((m, LANES), lambda j: (0, j)),
                   pl.BlockSpec((hist, s, LANES), lambda j: (0, 0, j))],
        out_shape=[jax.ShapeDtypeStruct((m, c), F32),
                   jax.ShapeDtypeStruct((hist, s, c), F32)],
        compiler_params=_params(1),
        name="conv_slab",
    )(v, *([st2d] * hist), w_dw, b_dw.reshape(1, c))


def _select_by_group(gid, vals):
    out = vals[-1]
    for g in range(len(vals) - 2, -1, -1):
        out = jnp.where(gid == g, vals[g], out)
    return out


def _pool_long_kernel(h_ref, st_ref, p_ref, nb_ref, xp_ref, s1_ref, s2_ref, s3_ref, s4_ref,
                      *, tps, start, lanes_per_group):
    tm, cb = h_ref.shape
    n = HIST_PAD + tm
    m = pl.program_id(1)
    first = (m % tps) == 0
    cols = [slice(j * LANES, (j + 1) * LANES) for j in range(cb // LANES)]

    @pl.when(first)
    def _():
        for j, cs in enumerate(cols):
            xp_ref[j, 0:HIST_PAD, :] = st_ref[0, :, cs]

    @pl.when(jnp.logical_not(first))
    def _():
        for j in range(len(cols)):
            xp_ref[j, 0:HIST_PAD, :] = xp_ref[j, tm:tm + HIST_PAD, :]

    for j, cs in enumerate(cols):
        gid = (pl.program_id(0) * len(cols) + j) // lanes_per_group
        xp_ref[j, HIST_PAD:n, :] = h_ref[:, cs]
        s1_ref[8:n, :] = xp_ref[j, 8:n, :] + xp_ref[j, 7:n - 1, :]
        s2_ref[16:n, :] = s1_ref[16:n, :] + s1_ref[14:n - 2, :]
        s3_ref[24:n, :] = s2_ref[24:n, :] + s2_ref[20:n - 4, :]
        s4_ref[32:n, :] = s3_ref[32:n, :] + s3_ref[24:n - 8, :]
        sums = [r[HIST_PAD:n, :] for r in (s1_ref, s2_ref, s3_ref, s4_ref)]
        if start >= max(POOL_WINDOWS) - 1:
            means = [sm * (1.0 / w) for sm, w in zip(sums, POOL_WINDOWS)]
        else:
            pos = start + (m % tps) * tm + lax.broadcasted_iota(jnp.int32, (tm, LANES), 0)
            means = [sm / jnp.minimum(w, pos + 1).astype(F32) for sm, w in zip(sums, POOL_WINDOWS)]
        p_ref[:, cs] = (_select_by_group(gid, means) - h_ref[:, cs]).astype(p_ref.dtype)
        nb_ref[0, :, cs] = xp_ref[j, tm:tm + HIST_PAD, :]


def _pool_long(h, st, *, n_seq, start):
    m, d = h.shape
    t = m // n_seq
    tm = _tile(t, ROWS_CONV, 16)
    assert tm >= HIST_PAD
    tps = t // tm
    lanes_per_group = d // len(POOL_WINDOWS) // LANES
    cb = _tile(d, COLS_CONV, LANES)
    return pl.pallas_call(
        functools.partial(_pool_long_kernel, tps=tps, start=start, lanes_per_group=lanes_per_group),
        grid=(d // cb, m // tm),
        in_specs=[pl.BlockSpec((tm, cb), lambda j, i: (i, j)),
                  pl.BlockSpec((1, HIST_PAD, cb), lambda j, i: (i // tps, 0, j))],
        out_specs=[pl.BlockSpec((tm, cb), lambda j, i: (i, j)),
                   pl.BlockSpec((1, HIST_PAD, cb), lambda j, i: (i // tps, 0, j))],
        out_shape=[jax.ShapeDtypeStruct((m, d), BF16),
                   jax.ShapeDtypeStruct((n_seq, HIST_PAD, d), F32)],
        scratch_shapes=[pltpu.VMEM((cb // LANES, HIST_PAD + tm, LANES), F32)]
        + [pltpu.VMEM((HIST_PAD + tm, LANES), F32) for _ in range(4)],
        compiler_params=_params(2),
        name="pool_long",
    )(h, st)


def _pool_slab_kernel(*refs, hist, n_tok, s, start, lanes_per_group):
    h_ref = refs[0]
    st_refs = refs[1:1 + hist]
    p_ref, nb_ref = refs[1 + hist:]
    gid = pl.program_id(0) // lanes_per_group

    def slab(i):
        if i < hist:
            return st_refs[i][...]
        return h_ref[(i - hist) * s:(i - hist + 1) * s, :]

    n = hist + n_tok
    level = [slab(i) for i in range(n)]
    levels = []
    step = 1
    for _ in POOL_WINDOWS:
        level = [level[i] + level[i - step] if i >= 2 * step - 1 else None for i in range(n)]
        levels.append(level)
        step *= 2
    for t in range(n_tok):
        means = [lv[hist + t] * (1.0 / min(w, start + t + 1)) for lv, w in zip(levels, POOL_WINDOWS)]
        p_ref[t * s:(t + 1) * s, :] = (_select_by_group(gid, means) - slab(hist + t)).astype(p_ref.dtype)
    for j in range(hist):
        nb_ref[j] = slab(j + n_tok)


def _pool_slab(h, st2d, *, s, start):
    m, d = h.shape
    hist = max(POOL_WINDOWS) - 1
    n_tok = m // s
    ncb = d // LANES
    lanes_per_group = d // len(POOL_WINDOWS) // LANES
    in_specs = [pl.BlockSpec((m, LANES), lambda j: (0, j))]
    in_specs += [pl.BlockSpec((s, LANES), lambda j, jj=jj: (0, jj * ncb + j)) for jj in range(hist)]
    return pl.pallas_call(
        functools.partial(_pool_slab_kernel, hist=hist, n_tok=n_tok, s=s, start=start,
                          lanes_per_group=lanes_per_group),
        grid=(ncb,),
        in_specs=in_specs,
        out_specs=[pl.BlockSpec((m, LANES), lambda j: (0, j)),
                   pl.BlockSpec((hist, s, LANES), lambda j: (0, 0, j))],
        out_shape=[jax.ShapeDtypeStruct((m, d), BF16),
                   jax.ShapeDtypeStruct((hist, s, d), F32)],
        compiler_params=_params(1),
        name="pool_slab",
    )(h, *([st2d] * hist))


def _grp_mm_kernel(p_ref, w_ref, sc_ref, res_ref, o_ref):
    y = jnp.dot(p_ref[...], w_ref[0], preferred_element_type=F32)
    o_ref[...] = res_ref[...] + y * sc_ref[...]


def _grp_mm(p, w_grp, layer, scale, res):
    m, d = p.shape
    _, ng, gc, _ = w_grp.shape
    tm = _tile(m, ROWS_MM, 16)
    return pl.pallas_call(
        _grp_mm_kernel,
        grid=(m // tm, ng),
        in_specs=[pl.BlockSpec((tm, gc), lambda i, g: (i, g)),
                  pl.BlockSpec((None, 1, gc, gc), lambda i, g: (layer, g, 0, 0)),
                  pl.BlockSpec((1, gc), lambda i, g: (0, g)),
                  pl.BlockSpec((tm, gc), lambda i, g: (i, g))],
        out_specs=pl.BlockSpec((tm, gc), lambda i, g: (i, g)),
        out_shape=jax.ShapeDtypeStruct((m, d), F32),
        compiler_params=_params(2),
        name="grp_mm",
    )(p, w_grp, scale.reshape(1, d), res)


def _cumsum_rows(x):
    n = x.shape[0]
    row = lax.broadcasted_iota(jnp.int32, x.shape, 0)
    d = 1
    while d < n:
        x = x + jnp.where(row >= d, pltpu.roll(x, d, axis=0), 0.0)
        d *= 2
    return x


def _level_table(c):
    t = np.arange(c)[:, None]
    s = np.arange(c)[None, :]
    x = np.maximum(t ^ s, 1)
    top = np.left_shift(1, np.floor(np.log2(x)).astype(np.int64))
    return np.where(s < t, top, np.where(s == t, 0, -1)).astype(np.int32)


def _level_ref(g_cum, length):
    c = g_cum.shape[0]
    if length >= SUBLANES:
        parts = [jnp.broadcast_to(g_cum[b + length - 1:b + length, :], (2 * length, HGRN_DK))
                 for b in range(0, c, 2 * length)]
        return parts[0] if len(parts) == 1 else jnp.concatenate(parts, axis=0)
    g3 = g_cum.reshape(c // SUBLANES, SUBLANES, HGRN_DK)
    sub = lax.broadcasted_iota(jnp.int32, g3.shape, 1)
    ref = None
    for b in range(SUBLANES - 2 * length, -1, -2 * length):
        cand = jnp.broadcast_to(g3[:, b + length - 1:b + length, :], g3.shape)
        ref = cand if ref is None else jnp.where(sub < b + 2 * length, cand, ref)
    return ref.reshape(c, HGRN_DK)


def _tile_pairs(q, kk, g_cum, inp):
    c = q.shape[0]
    t_idx = lax.broadcasted_iota(jnp.int32, (c, HGRN_DK), 0)
    ps = []
    for s in range(c):
        d = jnp.where(t_idx >= s, g_cum - g_cum[s:s + 1, :], -jnp.inf)
        ps.append(q * jnp.exp(d) * kk[s:s + 1, :])
    a_all = jnp.dot(jnp.concatenate(ps, axis=0), jnp.ones((HGRN_DK, LANES), F32),
                    preferred_element_type=F32)
    out = jnp.zeros((c, HGRN_DK), F32)
    for s in range(c):
        out = out + a_all[s * c:(s + 1) * c] * inp[s:s + 1, :]
    return out


def _hgrn_chunk(q_raw, fz, inp, lb, s0, lv):
    c = q_raw.shape[0]
    f = lb + (1.0 - lb) * _sigmoid(fz)
    kk = 1.0 - f
    q = _silu(q_raw)
    g_cum = _cumsum_rows(jnp.log(f))
    o = jnp.dot(q * jnp.exp(g_cum), s0, preferred_element_type=F32)
    if c == SUBLANES:
        o = o + _tile_pairs(q, kk, g_cum, inp)
    else:
        scores = jnp.zeros((c, c), F32)
        length = 0
        while length < c:
            if length == 0:
                qt, kt = q, kk
            else:
                e = jnp.exp(-jnp.abs(g_cum - _level_ref(g_cum, length)))
                qt, kt = q * e, kk * e
            sc = lax.dot_general(qt, kt, (((1,), (1,)), ((), ())), preferred_element_type=F32)
            scores = jnp.where(lv == length, sc, scores)
            length = max(1, 2 * length)
        o = o + jnp.dot(scores, inp, preferred_element_type=F32)
    g_end = g_cum[c - 1:c, :]
    kt = kk * jnp.exp(g_end - g_cum)
    upd = lax.dot_general(kt, inp, (((0,), (0,)), ((), ())), preferred_element_type=F32)
    decay = jnp.transpose(jnp.broadcast_to(jnp.exp(g_end), (HGRN_DK, HGRN_DK)))
    return o, decay * s0 + upd


def _hgrn_kernel(q_ref, f_ref, i_ref, g_ref, clb_ref, gn_ref, lv_ref, s0_ref, o_ref, sf_ref,
                 *, layer, hu, chunk):
    @pl.when(pl.program_id(2) == 0)
    def _():
        sf_ref[...] = s0_ref[...]

    clb = clb_ref[...]
    e = jnp.exp(clb - jnp.max(clb, axis=0, keepdims=True))
    sm = e / jnp.sum(e, axis=0, keepdims=True)
    lb_all = jnp.zeros_like(sm[0:1])
    for i in range(1, layer + 1):
        lb_all = lb_all + sm[i:i + 1]
    sb = sf_ref.shape[0]
    tb = q_ref.shape[0] // sb
    lv = lv_ref[...]
    for si in range(sb):
        for h in range(hu):
            hs = slice(h * HGRN_DK, (h + 1) * HGRN_DK)
            state = sf_ref[si, h]
            for cix in range(tb // chunk):
                rows = slice(si * tb + cix * chunk, si * tb + (cix + 1) * chunk)
                o, state = _hgrn_chunk(q_ref[rows, hs], f_ref[rows, hs], i_ref[rows, hs],
                                       lb_all[:, hs], state, lv)
                o = o * lax.rsqrt(jnp.mean(o * o, axis=-1, keepdims=True) + NORM_EPS)
                o_ref[rows, hs] = o * gn_ref[:, hs] * _silu(g_ref[rows, hs])
            sf_ref[si, h] = state


def _hgrn(proj, c_lb, g_norm, s0, *, n_seq, layer):
    m, d4 = proj.shape
    d = d4 // 4
    nh = d // HGRN_DK
    t = m // n_seq
    chunk = _tile(t, 128, SUBLANES)
    tb = _tile(t, 2 * chunk, chunk)
    short = t == tb == chunk == SUBLANES
    hu = nh if short else min(2, nh)
    sb = _tile(n_seq, 4, 1) if short else 1
    nhg = nh // hu
    ntb = t // tb
    wblk = hu * HGRN_DK
    rblk = sb * tb
    col = lambda part: (lambda b, hg, c: (b * ntb + c, part * nhg + hg))
    return pl.pallas_call(
        functools.partial(_hgrn_kernel, layer=layer, hu=hu, chunk=chunk),
        grid=(n_seq // sb, nhg, ntb),
        in_specs=[pl.BlockSpec((rblk, wblk), col(0)),
                  pl.BlockSpec((rblk, wblk), col(1)),
                  pl.BlockSpec((rblk, wblk), col(2)),
                  pl.BlockSpec((rblk, wblk), col(3)),
                  pl.BlockSpec((c_lb.shape[0], wblk), lambda b, hg, c: (0, hg)),
                  pl.BlockSpec((1, wblk), lambda b, hg, c: (0, hg)),
                  pl.BlockSpec((chunk, chunk), lambda b, hg, c: (0, 0)),
                  pl.BlockSpec((sb, hu, HGRN_DK, HGRN_DK), lambda b, hg, c: (b, hg, 0, 0))],
        out_specs=[pl.BlockSpec((rblk, wblk), lambda b, hg, c: (b * ntb + c, hg)),
                   pl.BlockSpec((sb, hu, HGRN_DK, HGRN_DK), lambda b, hg, c: (b, hg, 0, 0))],
        out_shape=[jax.ShapeDtypeStruct((m, d), F32),
                   jax.ShapeDtypeStruct(s0.shape, F32)],
        compiler_params=_params(3),
        name="hgrn",
    )(proj, proj, proj, proj, c_lb, g_norm.reshape(1, d), jnp.asarray(_level_table(chunk)), s0)


def _ffn_norm(x_ref, g_ref, h_ref):
    def chunk(rows):
        h_ref[rows, :] = _rms_rows(x_ref[rows, :], g_ref[...]).astype(BF16)

    _row_chunks(x_ref.shape[0], chunk)


def _ffn_up_long_kernel(x_ref, g_ref, wa_ref, wb_ref, dwa_ref, dwb_ref, ba_ref, bb_ref,
                        sta_ref, stb_ref, act_ref, nba_ref, nbb_ref,
                        h_ref, xpa_ref, xpb_ref, cra_ref, crb_ref, twa_ref, twb_ref, *, tps, nf):
    tm = x_ref.shape[0]
    tf = act_ref.shape[1]
    pad = FFN_HIST_PAD
    m = pl.program_id(0)
    f = pl.program_id(1)
    first = (m % tps) == 0
    cols = [slice(c * LANES, (c + 1) * LANES) for c in range(tf // LANES)]
    halves = ((xpa_ref, wa_ref, sta_ref, cra_ref, nba_ref), (xpb_ref, wb_ref, stb_ref, crb_ref, nbb_ref))
    rb = _tile(tm, max(tm // 4, 16), 16)
    rc = _tile(rb, 64, 16)
    n_blocks = tm // rb

    def norm(i):
        step = _tile(rb, 32, 16)
        for r0 in range(i * rb, (i + 1) * rb, step):
            h_ref[r0:r0 + step, :] = _rms_rows(x_ref[r0:r0 + step, :], g_ref[...]).astype(BF16)

    def fill_history(slot):
        @pl.when(first)
        def _():
            for xp_ref, _, st_ref, _, _ in halves:
                for c, cs in enumerate(cols):
                    xp_ref[slot, c, 0:pad, :] = st_ref[0, :, cs]

        @pl.when(jnp.logical_not(first))
        def _():
            for xp_ref, _, _, cr_ref, _ in halves:
                for c, cs in enumerate(cols):
                    xp_ref[slot, c, 0:pad, :] = cr_ref[f, :, cs]

    def project(slot, i):
        rows = slice(i * rb, (i + 1) * rb)
        for xp_ref, w_ref, _, _, _ in halves:
            u = jnp.dot(h_ref[rows, :], w_ref[...], preferred_element_type=F32)
            for c, cs in enumerate(cols):
                xp_ref[slot, c, pad + i * rb:pad + (i + 1) * rb, :] = u[:, cs]

    def save_tail(slot):
        for xp_ref, _, _, cr_ref, nb_ref in halves:
            for c, cs in enumerate(cols):
                tail = xp_ref[slot, c, tm:tm + pad, :]
                cr_ref[f, :, cs] = tail
                nb_ref[0, :, cs] = tail

    def load_taps():
        for tw_ref, dw_ref, b_ref in ((twa_ref, dwa_ref, ba_ref), (twb_ref, dwb_ref, bb_ref)):
            for k in range(3):
                tw_ref[k] = jnp.broadcast_to(dw_ref[k:k + 1, :], (pad, tf))
            tw_ref[3] = jnp.broadcast_to(b_ref[...], (pad, tf))

    def conv(xp_ref, tw_ref, slot, c, r0):
        def rows_back(k):
            return xp_ref[slot, c, r0 + pad - k:r0 + pad - k + rc, :].reshape(rc // pad, pad, LANES)

        tap = lambda k: tw_ref[k, :, cols[c]]
        return tap(0) * rows_back(2) + tap(1) * rows_back(1) + tap(2) * rows_back(0) + tap(3)

    def gate(slot, i):
        for r0 in range(i * rb, (i + 1) * rb, rc):
            for c, cs in enumerate(cols):
                ca = conv(xpa_ref, twa_ref, slot, c, r0)
                cb = conv(xpb_ref, twb_ref, slot, c, r0)
                act_ref[r0:r0 + rc, cs] = (_silu(ca) * cb).reshape(rc, LANES).astype(BF16)

    @pl.when(f == 0)
    def _():
        fill_history(0)
        for i in range(n_blocks):
            norm(i)
            project(0, i)
        save_tail(0)

    for slot in (0, 1):
        @pl.when(jnp.logical_and(jnp.logical_and(f > 0, f < nf), f % 2 == slot))
        def _():
            fill_history(slot)
            load_taps()
            for i in range(n_blocks):
                project(slot, i)
                gate(1 - slot, i)
            save_tail(slot)

    @pl.when(f == nf)
    def _():
        load_taps()
        for i in range(n_blocks):
            gate((nf - 1) % 2, i)


def _ffn_up_long(x, g, w_up, layer, w_dw, b_dw, st, *, n_seq):
    m, d = x.shape
    f2 = w_up.shape[2]
    ff = f2 // 2
    t = m // n_seq
    tm = _tile(t, ROWS_FFN, 16)
    tps = t // tm
    tf = _tile(ff, COLS_FFN, LANES)
    nf = ff // tf
    pad = FFN_HIST_PAD
    pj = lambda j: jnp.minimum(j, nf - 1)
    gj = lambda j: jnp.maximum(j - 1, 0)
    act, nba, nbb = pl.pallas_call(
        functools.partial(_ffn_up_long_kernel, tps=tps, nf=nf),
        grid=(m // tm, nf + 1),
        in_specs=[pl.BlockSpec((tm, d), lambda i, j: (i, 0)),
                  pl.BlockSpec((1, d), lambda i, j: (0, 0)),
                  pl.BlockSpec((None, d, tf), lambda i, j: (layer, 0, pj(j))),
                  pl.BlockSpec((None, d, tf), lambda i, j: (layer, 0, pj(j) + nf)),
                  pl.BlockSpec((3, tf), lambda i, j: (0, gj(j))),
                  pl.BlockSpec((3, tf), lambda i, j: (0, gj(j) + nf)),
                  pl.BlockSpec((1, tf), lambda i, j: (0, gj(j))),
                  pl.BlockSpec((1, tf), lambda i, j: (0, gj(j) + nf)),
                  pl.BlockSpec((1, pad, tf), lambda i, j: (i // tps, 0, pj(j))),
                  pl.BlockSpec((1, pad, tf), lambda i, j: (i // tps, 0, pj(j) + nf))],
        out_specs=[pl.BlockSpec((tm, tf), lambda i, j: (i, gj(j))),
                   pl.BlockSpec((1, pad, tf), lambda i, j: (i, 0, pj(j))),
                   pl.BlockSpec((1, pad, tf), lambda i, j: (i, 0, pj(j)))],
        out_shape=[jax.ShapeDtypeStruct((m, ff), BF16),
                   jax.ShapeDtypeStruct((m // tm, pad, ff), F32),
                   jax.ShapeDtypeStruct((m // tm, pad, ff), F32)],
        scratch_shapes=[pltpu.VMEM((tm, d), BF16),
                        pltpu.VMEM((2, tf // LANES, pad + tm, LANES), F32),
                        pltpu.VMEM((2, tf // LANES, pad + tm, LANES), F32),
                        pltpu.VMEM((nf, pad, tf), F32), pltpu.VMEM((nf, pad, tf), F32),
                        pltpu.VMEM((4, pad, tf), F32), pltpu.VMEM((4, pad, tf), F32)],
        compiler_params=_params(2),
        name="ffn_up_long",
    )(x, g.reshape(1, d), w_up, w_up, w_dw, w_dw, b_dw.reshape(1, f2), b_dw.reshape(1, f2), st, st)
    last = slice(tps - 1, None, tps)
    return act, jnp.concatenate([nba[last, pad - 2:], nbb[last, pad - 2:]], axis=-1)


def _ffn_up_slab_kernel(x_ref, g_ref, wa_ref, wb_ref, dwa_ref, dwb_ref, ba_ref, bb_ref,
                        sa0_ref, sa1_ref, sb0_ref, sb1_ref,
                        act_ref, na0_ref, na1_ref, nb0_ref, nb1_ref,
                        h_ref, xpa_ref, xpb_ref, *, s):
    m = x_ref.shape[0]

    @pl.when(pl.program_id(0) == 0)
    def _():
        _ffn_norm(x_ref, g_ref, h_ref)

    def half(w_ref, dw_ref, b_ref, s0_ref, s1_ref, n0_ref, n1_ref, xp_ref):
        u = jnp.dot(h_ref[...], w_ref[...], preferred_element_type=F32)
        xp_ref[0:s, :] = s0_ref[...]
        xp_ref[s:2 * s, :] = s1_ref[...]
        xp_ref[2 * s:2 * s + m, :] = u
        n0_ref[...] = xp_ref[m:m + s, :]
        n1_ref[...] = xp_ref[m + s:m + 2 * s, :]
        return (dw_ref[0:1, :] * xp_ref[0:m, :] + dw_ref[1:2, :] * xp_ref[s:s + m, :]
                + dw_ref[2:3, :] * u + b_ref[...])

    ca = half(wa_ref, dwa_ref, ba_ref, sa0_ref, sa1_ref, na0_ref, na1_ref, xpa_ref)
    cb = half(wb_ref, dwb_ref, bb_ref, sb0_ref, sb1_ref, nb0_ref, nb1_ref, xpb_ref)
    act_ref[...] = (_silu(ca) * cb).astype(BF16)


def _ffn_up_slab(x, g, w_up, layer, w_dw, b_dw, st2d, *, s):
    m, d = x.shape
    f2 = w_up.shape[2]
    ff = f2 // 2
    tf = _tile(ff, COLS_FFN, LANES)
    nf = ff // tf
    lo = lambda j: (0, j)
    hi = lambda j: (0, j + nf)
    outs = pl.pallas_call(
        functools.partial(_ffn_up_slab_kernel, s=s),
        grid=(nf,),
        in_specs=[pl.BlockSpec((m, d), lambda j: (0, 0)),
                  pl.BlockSpec((1, d), lambda j: (0, 0)),
                  pl.BlockSpec((None, d, tf), lambda j: (layer, 0, j)),
                  pl.BlockSpec((None, d, tf), lambda j: (layer, 0, j + nf)),
                  pl.BlockSpec((3, tf), lo), pl.BlockSpec((3, tf), hi),
                  pl.BlockSpec((1, tf), lo), pl.BlockSpec((1, tf), hi),
                  pl.BlockSpec((s, tf), lo), pl.BlockSpec((s, tf), lambda j: (0, j + 2 * nf)),
                  pl.BlockSpec((s, tf), hi), pl.BlockSpec((s, tf), lambda j: (0, j + 3 * nf))],
        out_specs=[pl.BlockSpec((m, tf), lo)] + [pl.BlockSpec((s, tf), lo)] * 4,
        out_shape=[jax.ShapeDtypeStruct((m, ff), BF16)] + [jax.ShapeDtypeStruct((s, ff), F32)] * 4,
        scratch_shapes=[pltpu.VMEM((m, d), BF16),
                        pltpu.VMEM((2 * s + m, tf), F32), pltpu.VMEM((2 * s + m, tf), F32)],
        compiler_params=_params(1),
        name="ffn_up_slab",
    )(x, g.reshape(1, d), w_up, w_up, w_dw, w_dw, b_dw.reshape(1, f2), b_dw.reshape(1, f2),
      st2d, st2d, st2d, st2d)
    act, na0, na1, nb0, nb1 = outs
    new = jnp.stack([jnp.concatenate([na0, nb0], axis=-1), jnp.concatenate([na1, nb1], axis=-1)], axis=1)
    return act, new


def _front_pad(st, pad):
    return jnp.pad(st, ((0, 0), (pad - st.shape[1], 0), (0, 0)))


def _trunk(x, st_a, st_b, st_c, st_f, start, wts, *, slab):
    (norm_mix, norm_ffn, norm_final, a_w_pw1, a_w_dw, a_b_dw, a_ln_g, a_ln_b, a_w_pw2,
     b_w_grp, b_scale, c_lb, c_w_in, c_g_norm, c_w_o, f_w_up, f_w_dw, f_b_dw, f_w_down) = wts
    bsz, t, d = x.shape
    depth = norm_mix.shape[0]
    m = bsz * t
    if slab:
        to_rows = lambda a: a.transpose(1, 0, 2).reshape(m, a.shape[-1])
        from_rows = lambda a: a.reshape(t, bsz, a.shape[-1]).transpose(1, 0, 2)
    else:
        to_rows = lambda a: a.reshape(m, a.shape[-1])
        from_rows = lambda a: a.reshape(bsz, t, a.shape[-1])
    xr = to_rows(x)
    new_a, new_b, new_c, new_f = [], [], [], []
    for layer in range(depth):
        kind, j = layer % 3, layer // 3
        if kind == 0:
            v = _pro_mm(xr, a_w_pw1, j, mode="rms", p0=norm_mix[layer], glu=True, name="a_pw1")
            hist = a_w_dw.shape[1] - 1
            if slab:
                c_pre, nb = _conv_slab(v, st_a[j].reshape(bsz, -1), a_w_dw[j], a_b_dw[j], s=bsz)
                nb = nb.transpose(1, 0, 2)
            else:
                c_pre, nb = _conv_long(v, _front_pad(st_a[j], HIST_PAD), a_w_dw[j], a_b_dw[j], n_seq=bsz)
                nb = nb[:, HIST_PAD - hist:]
            new_a.append(nb)
            xr = _pro_mm(c_pre, a_w_pw2, j, mode="ln_silu", p0=a_ln_g[j], p1=a_ln_b[j], res=xr, name="a_pw2")
        elif kind == 1:
            h = _rms(xr, norm_mix[layer])
            hist = max(POOL_WINDOWS) - 1
            if slab:
                pooled, nb = _pool_slab(h, st_b[j].reshape(bsz, -1), s=bsz, start=start)
                nb = nb.transpose(1, 0, 2)
            else:
                pooled, nb = _pool_long(h, _front_pad(st_b[j], HIST_PAD), n_seq=bsz, start=start)
                nb = nb[:, HIST_PAD - hist:]
            new_b.append(nb)
            xr = _grp_mm(pooled, b_w_grp, j, b_scale[j], xr)
        else:
            proj = _pro_mm(xr, c_w_in, j, mode="rms", p0=norm_mix[layer], name="c_in")
            if slab:
                proj = from_rows(proj).reshape(m, proj.shape[-1])
            og, s_fin = _hgrn(proj, c_lb, c_g_norm[j], st_c[j], n_seq=bsz, layer=layer)
            if slab:
                og = to_rows(og.reshape(bsz, t, d))
            new_c.append(s_fin)
            xr = _pro_mm(og, c_w_o, j, mode="cast", res=xr, name="c_out")
        if slab:
            act, nb = _ffn_up_slab(xr, norm_ffn[layer], f_w_up, layer, f_w_dw[layer], f_b_dw[layer],
                                   st_f[layer].reshape(bsz, -1), s=bsz)
        else:
            act, nb = _ffn_up_long(xr, norm_ffn[layer], f_w_up, layer, f_w_dw[layer], f_b_dw[layer],
                                   _front_pad(st_f[layer], FFN_HIST_PAD), n_seq=bsz)
        new_f.append(nb)
        xr = _pro_mm(act, f_w_down, layer, mode="cast", res=xr, name="f_down")
    y = from_rows(_rms(xr, norm_final))
    return y, jnp.stack(new_a), jnp.stack(new_b), jnp.stack(new_c), jnp.stack(new_f)


def kernel(x_prompt, x_sample, state_conv_a, state_pool, state_hgrn, state_ffn_conv, norm_mix, norm_ffn, norm_final, a_w_pw1, a_w_dw, a_b_dw, a_ln_g, a_ln_b, a_w_pw2, b_w_grp, b_scale, c_lb, c_w_in, c_g_norm, c_w_o, f_w_up, f_w_dw, f_b_dw, f_w_down):
    bf = lambda w: w.astype(BF16)
    wts = (norm_mix, norm_ffn, norm_final, bf(a_w_pw1), a_w_dw, a_b_dw, a_ln_g, a_ln_b, bf(a_w_pw2),
           bf(b_w_grp), b_scale, c_lb, bf(c_w_in), c_g_norm, bf(c_w_o), bf(f_w_up), f_w_dw, f_b_dw,
           bf(f_w_down))
    bp = x_prompt.shape[0]
    zeros = lambda st: jnp.zeros((st.shape[0], bp) + st.shape[2:], st.dtype)
    yp, pa, pb, pc, pf = _trunk(x_prompt, zeros(state_conv_a), zeros(state_pool), zeros(state_hgrn),
                                zeros(state_ffn_conv), 0, wts, slab=False)
    ys, sa, sb, sc, sf = _trunk(x_sample, state_conv_a, state_pool, state_hgrn, state_ffn_conv,
                                PAST_LEN, wts, slab=True)
    return (yp, ys, pa, sa, pb, sb, pc, sc, pf, sf)
```

```python
import functools

import numpy as np
import jax
import jax.numpy as jnp
from jax import lax
from jax.experimental import pallas as pl
from jax.experimental.pallas import tpu as pltpu

F32 = jnp.float32
BF16 = jnp.bfloat16

NORM_EPS = 1e-6
POOL_WINDOWS = (2, 4, 8, 16)
HGRN_DK = 128
LANES = 128
SUBLANES = 8
HIST_PAD = 32
FFN_HIST_PAD = 8
VMEM_LIMIT_BYTES = 56 * 1024 * 1024
PAST_LEN = 16384
ROWS_MM = 1024
COLS_MM = 1024
COLS_FFN = 512
MM_VMEM_BUDGET = 46 * 1024 * 1024
ROWS_CONV = 512
COLS_CONV = 512
ROWS_FFN = 1024


def _params(n_axes):
    return pltpu.CompilerParams(dimension_semantics=("arbitrary",) * n_axes,
                                vmem_limit_bytes=VMEM_LIMIT_BYTES)


def _tile(n, pref, mult):
    t = min(pref, n)
    t -= t % mult
    while t >= mult:
        if n % t == 0:
            return t
        t -= mult
    return n


def _sigmoid(x):
    return 1.0 / (1.0 + jnp.exp(-x))


def _silu(x):
    return x * _sigmoid(x)


def _rms_rows(x, g):
    ms = jnp.mean(x * x, axis=-1, keepdims=True)
    return x * lax.rsqrt(ms + NORM_EPS) * g


def _ln_silu_rows(x, g, b):
    mu = jnp.mean(x, axis=-1, keepdims=True)
    d = x - mu
    var = jnp.mean(d * d, axis=-1, keepdims=True)
    return _silu(d * lax.rsqrt(var + NORM_EPS) * g + b)


def _row_chunks(tm, fn):
    rc = _tile(tm, 128, SUBLANES)

    def body(i, carry):
        fn(pl.ds(pl.multiple_of(i * rc, rc), rc))
        return carry

    lax.fori_loop(0, tm // rc, body, 0)


def _rms_kernel(x_ref, g_ref, o_ref):
    tm = x_ref.shape[0]

    def chunk(rows):
        o_ref[rows, :] = _rms_rows(x_ref[rows, :], g_ref[...])

    _row_chunks(tm, chunk)


def _rms(x, g):
    m, d = x.shape
    tm = _tile(m, 512, SUBLANES)
    return pl.pallas_call(
        _rms_kernel,
        grid=(m // tm,),
        in_specs=[pl.BlockSpec((tm, d), lambda i: (i, 0)),
                  pl.BlockSpec((1, d), lambda i: (0, 0))],
        out_specs=pl.BlockSpec((tm, d), lambda i: (i, 0)),
        out_shape=jax.ShapeDtypeStruct((m, d), F32),
        compiler_params=_params(1),
        name="rms",
    )(x, g.reshape(1, d))


def _forget_floor(clb, layer):
    e = jnp.exp(clb - jnp.max(clb, axis=0, keepdims=True))
    sm = e / jnp.sum(e, axis=0, keepdims=True)
    lb = jnp.zeros_like(sm[0:1])
    for i in range(1, layer + 1):
        lb = lb + sm[i:i + 1]
    return lb


def _pro_mm_kernel(*refs, mode, glu, has_res, post, post_layer):
    it = iter(refs)
    x_ref = next(it)
    p0_ref = next(it) if mode in ("rms", "ln_silu") else None
    p1_ref = next(it) if mode == "ln_silu" else None
    w_ref = next(it)
    wg_ref = next(it) if glu else None
    res_ref = next(it) if has_res else None
    e_ref = next(it) if post in ("forget", "silu_scale") else None
    o_ref = next(it)
    tm = x_ref.shape[0]

    a_ref = x_ref if x_ref.dtype == BF16 else next(it)
    rb = tm if a_ref is x_ref else _tile(tm, max(tm // 4, 16), 16)

    def prologue(rows0):
        step = _tile(rb, 32, 16)
        for r0 in range(rows0, rows0 + rb, step):
            xv = x_ref[r0:r0 + step, :]
            if mode == "rms":
                xv = _rms_rows(xv, p0_ref[...])
            elif mode == "ln_silu":
                xv = _ln_silu_rows(xv, p0_ref[...], p1_ref[...])
            a_ref[r0:r0 + step, :] = xv.astype(BF16)

    def product(rows0):
        rows = slice(rows0, rows0 + rb)
        a = a_ref[rows, :]
        acc = jnp.dot(a, w_ref[...], preferred_element_type=F32)
        if glu:
            acc = acc * _sigmoid(jnp.dot(a, wg_ref[...], preferred_element_type=F32))
        if has_res:
            acc = res_ref[rows, :] + acc
        if post == "silu":
            acc = _silu(acc)
        elif post == "silu_scale":
            acc = _silu(acc) * e_ref[...]
        elif post == "forget":
            lb = _forget_floor(e_ref[...], post_layer)
            acc = lb + (1.0 - lb) * _sigmoid(acc)
        o_ref[rows, :] = acc

    if a_ref is x_ref:
        for rows0 in range(0, tm, rb):
            product(rows0)
    else:
        @pl.when(pl.program_id(1) == 0)
        def _():
            for rows0 in range(0, tm, rb):
                prologue(rows0)
                product(rows0)

        @pl.when(pl.program_id(1) > 0)
        def _():
            for rows0 in range(0, tm, rb):
                product(rows0)


def _mm_vmem_bytes(tm, k, tn, x_bytes, glu, has_res):
    n_w = 2 if glu else 1
    blocks = 2 * (tm * k * x_bytes + n_w * k * tn * 2 + (2 if has_res else 1) * tm * tn * 4)
    return blocks + (tm * k * 2 if x_bytes == 4 else 0) + (n_w + 1) * tm * tn * 4


def _pro_mm(x, w, layer, *, mode, p0=None, p1=None, glu=False, res=None, cols=None,
            post=None, post_arg=None, post_layer=0, name):
    m, k = x.shape
    c0, n = cols if cols is not None else (0, w.shape[2] // (2 if glu else 1))
    tm = _tile(m, ROWS_MM, 16)
    tn = _tile(n, COLS_MM, LANES)
    while tn > LANES and _mm_vmem_bytes(tm, k, tn, x.dtype.itemsize, glu, res is not None) > MM_VMEM_BUDGET:
        tn = _tile(n, tn // 2, LANES)
    nn = n // tn
    args = [x]
    in_specs = [pl.BlockSpec((tm, k), lambda i, j: (i, 0))]
    for p in (p0, p1):
        if p is not None:
            args.append(p.reshape(1, k))
            in_specs.append(pl.BlockSpec((1, k), lambda i, j: (0, 0)))
    assert c0 % tn == 0
    j0 = c0 // tn
    args.append(w)
    in_specs.append(pl.BlockSpec((None, k, tn), lambda i, j: (layer, 0, j0 + j)))
    if glu:
        args.append(w)
        in_specs.append(pl.BlockSpec((None, k, tn), lambda i, j: (layer, 0, j0 + j + nn)))
    if res is not None:
        args.append(res)
        in_specs.append(pl.BlockSpec((tm, tn), lambda i, j: (i, j)))
    if post == "silu_scale":
        args.append(post_arg.reshape(1, n))
        in_specs.append(pl.BlockSpec((1, tn), lambda i, j: (0, j)))
    elif post == "forget":
        args.append(post_arg)
        in_specs.append(pl.BlockSpec((post_arg.shape[0], tn), lambda i, j: (0, j)))
    return pl.pallas_call(
        functools.partial(_pro_mm_kernel, mode=mode, glu=glu, has_res=res is not None,
                          post=post, post_layer=post_layer),
        grid=(m // tm, nn),
        in_specs=in_specs,
        out_specs=pl.BlockSpec((tm, tn), lambda i, j: (i, j)),
        out_shape=jax.ShapeDtypeStruct((m, n), F32),
        scratch_shapes=[] if x.dtype == BF16 else [pltpu.VMEM((tm, k), BF16)],
        compiler_params=_params(2),
        name=name,
    )(*args)


def _conv_long_kernel(v_ref, st_ref, w_ref, b_ref, c_ref, nb_ref, xp_ref, wk_ref, cs_ref, *, tps):
    tm, cb = v_ref.shape
    width = w_ref.shape[0]
    off = HIST_PAD - (width - 1)
    rc = _tile(tm, 128, SUBLANES)
    m = pl.program_id(1)
    start = (m % tps) == 0
    cols = [slice(j * LANES, (j + 1) * LANES) for j in range(cb // LANES)]

    @pl.when(start)
    def _():
        for j, cs in enumerate(cols):
            xp_ref[j, 0:HIST_PAD, :] = st_ref[0, :, cs]

    @pl.when(jnp.logical_not(start))
    def _():
        for j in range(len(cols)):
            xp_ref[j, 0:HIST_PAD, :] = xp_ref[j, tm:tm + HIST_PAD, :]

    for j, cs in enumerate(cols):
        xp_ref[j, HIST_PAD:HIST_PAD + tm, :] = v_ref[:, cs]
        wk_ref[j, 0:width, :] = w_ref[:, cs]
        wk_ref[j, width:width + 1, :] = b_ref[:, cs]

    def column(j, carry):
        for r in range(tm // rc):
            acc = jnp.broadcast_to(wk_ref[j, width:width + 1, :], (rc, LANES))
            for rho in range(min(SUBLANES, width)):
                taps = range(rho, width, SUBLANES)
                lo = r * rc + off + rho
                win = xp_ref[j, lo:lo + rc + SUBLANES * (len(taps) - 1), :]
                for i, k in enumerate(taps):
                    acc = acc + wk_ref[j, k:k + 1, :] * win[SUBLANES * i:SUBLANES * i + rc]
            cs_ref[j, r * rc:(r + 1) * rc, :] = acc
        return carry

    lax.fori_loop(0, len(cols), column, 0)
    for j, cs in enumerate(cols):
        c_ref[:, cs] = cs_ref[j]
        nb_ref[0, :, cs] = xp_ref[j, tm:tm + HIST_PAD, :]


def _conv_long(v, st, w_dw, b_dw, *, n_seq):
    m, c = v.shape
    width = w_dw.shape[0]
    t = m // n_seq
    tm = _tile(t, ROWS_CONV, SUBLANES)
    assert tm >= HIST_PAD
    tps = t // tm
    cb = _tile(c, COLS_CONV, LANES)
    ncl = cb // LANES
    return pl.pallas_call(
        functools.partial(_conv_long_kernel, tps=tps),
        grid=(c // cb, m // tm),
        in_specs=[pl.BlockSpec((tm, cb), lambda j, i: (i, j)),
                  pl.BlockSpec((1, HIST_PAD, cb), lambda j, i: (i // tps, 0, j)),
                  pl.BlockSpec((width, cb), lambda j, i: (0, j)),
                  pl.BlockSpec((1, cb), lambda j, i: (0, j))],
        out_specs=[pl.BlockSpec((tm, cb), lambda j, i: (i, j)),
                   pl.BlockSpec((1, HIST_PAD, cb), lambda j, i: (i // tps, 0, j))],
        out_shape=[jax.ShapeDtypeStruct((m, c), F32),
                   jax.ShapeDtypeStruct((n_seq, HIST_PAD, c), F32)],
        scratch_shapes=[pltpu.VMEM((ncl, HIST_PAD + tm, LANES), F32),
                        pltpu.VMEM((ncl, width + 1, LANES), F32),
                        pltpu.VMEM((ncl, tm, LANES), F32)],
        compiler_params=_params(2),
        name="conv_long",
    )(v, st, w_dw, b_dw.reshape(1, c))


def _conv_slab_kernel(*refs, hist, n_tok, s):
    v_ref = refs[0]
    st_refs = refs[1:1 + hist]
    w_ref, b_ref, c_ref, nb_ref = refs[1 + hist:]
    width = w_ref.shape[0]

    def slab(i):
        if i < hist:
            return st_refs[i][...]
        return v_ref[(i - hist) * s:(i - hist + 1) * s, :]

    for t in range(n_tok):
        acc = jnp.broadcast_to(b_ref[...], (s, LANES))
        for k in range(width):
            acc = acc + w_ref[k:k + 1, :] * slab(t + k)
        c_ref[t * s:(t + 1) * s, :] = acc
    for j in range(hist):
        nb_ref[j] = slab(j + n_tok)


def _conv_slab(v, st2d, w_dw, b_dw, *, s):
    m, c = v.shape
    width = w_dw.shape[0]
    hist = width - 1
    n_tok = m // s
    ncb = c // LANES
    in_specs = [pl.BlockSpec((m, LANES), lambda j: (0, j))]
    in_specs += [pl.BlockSpec((s, LANES), lambda j, jj=jj: (0, jj * ncb + j)) for jj in range(hist)]
    in_specs += [pl.BlockSpec((width, LANES), lambda j: (0, j)),
                 pl.BlockSpec((1, LANES), lambda j: (0, j))]
    return pl.pallas_call(
        functools.partial(_conv_slab_kernel, hist=hist, n_tok=n_tok, s=s),
        grid=(ncb,),
        in_specs=in_specs,
        out_specs=[pl.BlockSpec((m, LANES), lambda j: (0, j)),
                   pl.BlockSpec((hist, s, LANES), lambda j: (0, 0, j))],
        out_shape=[jax.ShapeDtypeStruct((m, c), F32),
                   jax.ShapeDtypeStruct((hist, s, c), F32)],
        compiler_params=_params(1),
        name="conv_slab",
    )(v, *([st2d] * hist), w_dw, b_dw.reshape(1, c))


def _select_by_group(gid, vals):
    out = vals[-1]
    for g in range(len(vals) - 2, -1, -1):
        out = jnp.where(gid == g, vals[g], out)
    return out


def _pool_long_kernel(h_ref, st_ref, p_ref, nb_ref, xp_ref, s1_ref, s2_ref, s3_ref, s4_ref,
                      *, tps, start, lanes_per_group):
    tm, cb = h_ref.shape
    n = HIST_PAD + tm
    m = pl.program_id(1)
    first = (m % tps) == 0
    cols = [slice(j * LANES, (j + 1) * LANES) for j in range(cb // LANES)]

    @pl.when(first)
    def _():
        for j, cs in enumerate(cols):
            xp_ref[j, 0:HIST_PAD, :] = st_ref[0, :, cs]

    @pl.when(jnp.logical_not(first))
    def _():
        for j in range(len(cols)):
            xp_ref[j, 0:HIST_PAD, :] = xp_ref[j, tm:tm + HIST_PAD, :]

    for j, cs in enumerate(cols):
        gid = (pl.program_id(0) * len(cols) + j) // lanes_per_group
        xp_ref[j, HIST_PAD:n, :] = h_ref[:, cs]
        s1_ref[8:n, :] = xp_ref[j, 8:n, :] + xp_ref[j, 7:n - 1, :]
        s2_ref[16:n, :] = s1_ref[16:n, :] + s1_ref[14:n - 2, :]
        s3_ref[24:n, :] = s2_ref[24:n, :] + s2_ref[20:n - 4, :]
        s4_ref[32:n, :] = s3_ref[32:n, :] + s3_ref[24:n - 8, :]
        sums = [r[HIST_PAD:n, :] for r in (s1_ref, s2_ref, s3_ref, s4_ref)]
        if start >= max(POOL_WINDOWS) - 1:
            means = [sm * (1.0 / w) for sm, w in zip(sums, POOL_WINDOWS)]
        else:
            pos = start + (m % tps) * tm + lax.broadcasted_iota(jnp.int32, (tm, LANES), 0)
            means = [sm / jnp.minimum(w, pos + 1).astype(F32) for sm, w in zip(sums, POOL_WINDOWS)]
        p_ref[:, cs] = (_select_by_group(gid, means) - h_ref[:, cs]).astype(p_ref.dtype)
        nb_ref[0, :, cs] = xp_ref[j, tm:tm + HIST_PAD, :]


def _pool_long(h, st, *, n_seq, start):
    m, d = h.shape
    t = m // n_seq
    tm = _tile(t, ROWS_CONV, 16)
    assert tm >= HIST_PAD
    tps = t // tm
    lanes_per_group = d // len(POOL_WINDOWS) // LANES
    cb = _tile(d, COLS_CONV, LANES)
    return pl.pallas_call(
        functools.partial(_pool_long_kernel, tps=tps, start=start, lanes_per_group=lanes_per_group),
        grid=(d // cb, m // tm),
        in_specs=[pl.BlockSpec((tm, cb), lambda j, i: (i, j)),
                  pl.BlockSpec((1, HIST_PAD, cb), lambda j, i: (i // tps, 0, j))],
        out_specs=[pl.BlockSpec((tm, cb), lambda j, i: (i, j)),
                   pl.BlockSpec((1, HIST_PAD, cb), lambda j, i: (i // tps, 0, j))],
        out_shape=[jax.ShapeDtypeStruct((m, d), BF16),
                   jax.ShapeDtypeStruct((n_seq, HIST_PAD, d), F32)],
        scratch_shapes=[pltpu.VMEM((cb // LANES, HIST_PAD + tm, LANES), F32)]
        + [pltpu.VMEM((HIST_PAD + tm, LANES), F32) for _ in range(4)],
        compiler_params=_params(2),
        name="pool_long",
    )(h, st)


def _pool_slab_kernel(*refs, hist, n_tok, s, start, lanes_per_group):
    h_ref = refs[0]
    st_refs = refs[1:1 + hist]
    p_ref, nb_ref = refs[1 + hist:]
    gid = pl.program_id(0) // lanes_per_group

    def slab(i):
        if i < hist:
            return st_refs[i][...]
        return h_ref[(i - hist) * s:(i - hist + 1) * s, :]

    n = hist + n_tok
    level = [slab(i) for i in range(n)]
    levels = []
    step = 1
    for _ in POOL_WINDOWS:
        level = [level[i] + level[i - step] if i >= 2 * step - 1 else None for i in range(n)]
        levels.append(level)
        step *= 2
    for t in range(n_tok):
        means = [lv[hist + t] * (1.0 / min(w, start + t + 1)) for lv, w in zip(levels, POOL_WINDOWS)]
        p_ref[t * s:(t + 1) * s, :] = (_select_by_group(gid, means) - slab(hist + t)).astype(p_ref.dtype)
    for j in range(hist):
        nb_ref[j] = slab(j + n_tok)


def _pool_slab(h, st2d, *, s, start):
    m, d = h.shape
    hist = max(POOL_WINDOWS) - 1
    n_tok = m // s
    ncb = d // LANES
    lanes_per_group = d // len(POOL_WINDOWS) // LANES
    in_specs = [pl.BlockSpec((m, LANES), lambda j: (0, j))]
    in_specs += [pl.BlockSpec((s, LANES), lambda j, jj=jj: (0, jj * ncb + j)) for jj in range(hist)]
    return pl.pallas_call(
        functools.partial(_pool_slab_kernel, hist=hist, n_tok=n_tok, s=s, start=start,
                          lanes_per_group=lanes_per_group),
        grid=(ncb,),
        in_specs=in_specs,
        out_specs=[pl.BlockSpec((m, LANES), lambda j: (0, j)),
                   pl.BlockSpec((hist, s, LANES), lambda j: (0, 0, j))],
        out_shape=[jax.ShapeDtypeStruct((m, d), BF16),
                   jax.ShapeDtypeStruct((hist, s, d), F32)],
        compiler_params=_params(1),
        name="pool_slab",
    )(h, *([st2d] * hist))


def _grp_mm_kernel(p_ref, w_ref, sc_ref, res_ref, o_ref):
    y = jnp.dot(p_ref[...], w_ref[0], preferred_element_type=F32)
    o_ref[...] = res_ref[...] + y * sc_ref[...]


def _grp_mm(p, w_grp, layer, scale, res):
    m, d = p.shape
    _, ng, gc, _ = w_grp.shape
    tm = _tile(m, ROWS_MM, 16)
    return pl.pallas_call(
        _grp_mm_kernel,
        grid=(m // tm, ng),
        in_specs=[pl.BlockSpec((tm, gc), lambda i, g: (i, g)),
                  pl.BlockSpec((None, 1, gc, gc), lambda i, g: (layer, g, 0, 0)),
                  pl.BlockSpec((1, gc), lambda i, g: (0, g)),
                  pl.BlockSpec((tm, gc), lambda i, g: (i, g))],
        out_specs=pl.BlockSpec((tm, gc), lambda i, g: (i, g)),
        out_shape=jax.ShapeDtypeStruct((m, d), F32),
        compiler_params=_params(2),
        name="grp_mm",
    )(p, w_grp, scale.reshape(1, d), res)


def _cumsum_rows(x):
    n = x.shape[0]
    row = lax.broadcasted_iota(jnp.int32, x.shape, 0)
    d = 1
    while d < n:
        x = x + jnp.where(row >= d, pltpu.roll(x, d, axis=0), 0.0)
        d *= 2
    return x


def _level_table(c):
    t = np.arange(c)[:, None]
    s = np.arange(c)[None, :]
    x = np.maximum(t ^ s, 1)
    top = np.left_shift(1, np.floor(np.log2(x)).astype(np.int64))
    return np.where(s < t, top, np.where(s == t, 0, -1)).astype(np.int32)


def _level_ref(g_cum, length):
    c = g_cum.shape[0]
    if length >= SUBLANES:
        parts = [jnp.broadcast_to(g_cum[b + length - 1:b + length, :], (2 * length, HGRN_DK))
                 for b in range(0, c, 2 * length)]
        return parts[0] if len(parts) == 1 else jnp.concatenate(parts, axis=0)
    g3 = g_cum.reshape(c // SUBLANES, SUBLANES, HGRN_DK)
    sub = lax.broadcasted_iota(jnp.int32, g3.shape, 1)
    ref = None
    for b in range(SUBLANES - 2 * length, -1, -2 * length):
        cand = jnp.broadcast_to(g3[:, b + length - 1:b + length, :], g3.shape)
        ref = cand if ref is None else jnp.where(sub < b + 2 * length, cand, ref)
    return ref.reshape(c, HGRN_DK)


def _tile_pairs(q, kk, g_cum, inp):
    c = q.shape[0]
    t_idx = lax.broadcasted_iota(jnp.int32, (c, HGRN_DK), 0)
    ps = []
    for s in range(c):
        d = jnp.where(t_idx >= s, g_cum - g_cum[s:s + 1, :], -jnp.inf)
        ps.append(q * jnp.exp(d) * kk[s:s + 1, :])
    a_all = jnp.dot(jnp.concatenate(ps, axis=0), jnp.ones((HGRN_DK, LANES), F32),
                    preferred_element_type=F32)
    out = jnp.zeros((c, HGRN_DK), F32)
    for s in range(c):
        out = out + a_all[s * c:(s + 1) * c] * inp[s:s + 1, :]
    return out


def _hgrn_chunk(q, f, inp, s0, lv):
    c = q.shape[0]
    kk = 1.0 - f
    g_cum = _cumsum_rows(jnp.log(f))
    o = jnp.dot(q * jnp.exp(g_cum), s0, preferred_element_type=F32)
    if c == SUBLANES:
        o = o + _tile_pairs(q, kk, g_cum, inp)
    else:
        scores = jnp.zeros((c, c), F32)
        length = 0
        while length < c:
            if length == 0:
                qt, kt = q, kk
            else:
                e = jnp.exp(-jnp.abs(g_cum - _level_ref(g_cum, length)))
                qt, kt = q * e, kk * e
            sc = lax.dot_general(qt, kt, (((1,), (1,)), ((), ())), preferred_element_type=F32)
            scores = jnp.where(lv == length, sc, scores)
            length = max(1, 2 * length)
        o = o + jnp.dot(scores, inp, preferred_element_type=F32)
    g_end = g_cum[c - 1:c, :]
    kt = kk * jnp.exp(g_end - g_cum)
    upd = lax.dot_general(kt, inp, (((0,), (0,)), ((), ())), preferred_element_type=F32)
    decay = jnp.transpose(jnp.broadcast_to(jnp.exp(g_end), (HGRN_DK, HGRN_DK)))
    return o, decay * s0 + upd


def _hgrn_kernel(q_ref, f_ref, i_ref, g_ref, lv_ref, s0_ref, o_ref, sf_ref, *, hu, chunk):
    @pl.when(pl.program_id(2) == 0)
    def _():
        sf_ref[...] = s0_ref[...]

    sb = sf_ref.shape[0]
    tb = q_ref.shape[0] // sb
    lv = lv_ref[...]
    for si in range(sb):
        for h in range(hu):
            hs = slice(h * HGRN_DK, (h + 1) * HGRN_DK)
            state = sf_ref[si, h]
            for cix in range(tb // chunk):
                rows = slice(si * tb + cix * chunk, si * tb + (cix + 1) * chunk)
                o, state = _hgrn_chunk(q_ref[rows, hs], f_ref[rows, hs], i_ref[rows, hs], state, lv)
                o = o * lax.rsqrt(jnp.mean(o * o, axis=-1, keepdims=True) + NORM_EPS)
                o_ref[rows, hs] = o * g_ref[rows, hs]
            sf_ref[si, h] = state


def _hgrn(q, f, inp, gate, s0, *, n_seq):
    m, d = q.shape
    nh = d // HGRN_DK
    t = m // n_seq
    chunk = _tile(t, 128, SUBLANES)
    tb = _tile(t, 2 * chunk, chunk)
    short = t == tb == chunk == SUBLANES
    hu = nh if short else min(2, nh)
    sb = _tile(n_seq, 4, 1) if short else 1
    nhg = nh // hu
    ntb = t // tb
    wblk = hu * HGRN_DK
    rblk = sb * tb
    rows = pl.BlockSpec((rblk, wblk), lambda b, hg, c: (b * ntb + c, hg))
    return pl.pallas_call(
        functools.partial(_hgrn_kernel, hu=hu, chunk=chunk),
        grid=(n_seq // sb, nhg, ntb),
        in_specs=[rows, rows, rows, rows,
                  pl.BlockSpec((chunk, chunk), lambda b, hg, c: (0, 0)),
                  pl.BlockSpec((sb, hu, HGRN_DK, HGRN_DK), lambda b, hg, c: (b, hg, 0, 0))],
        out_specs=[pl.BlockSpec((rblk, wblk), lambda b, hg, c: (b * ntb + c, hg)),
                   pl.BlockSpec((sb, hu, HGRN_DK, HGRN_DK), lambda b, hg, c: (b, hg, 0, 0))],
        out_shape=[jax.ShapeDtypeStruct((m, d), F32),
                   jax.ShapeDtypeStruct(s0.shape, F32)],
        compiler_params=_params(3),
        name="hgrn",
    )(q, f, inp, gate, jnp.asarray(_level_table(chunk)), s0)


def _ffn_norm(x_ref, g_ref, h_ref):
    def chunk(rows):
        h_ref[rows, :] = _rms_rows(x_ref[rows, :], g_ref[...]).astype(BF16)

    _row_chunks(x_ref.shape[0], chunk)


def _ffn_up_long_kernel(x_ref, g_ref, wa_ref, wb_ref, dwa_ref, dwb_ref, ba_ref, bb_ref,
                        sta_ref, stb_ref, act_ref, nba_ref, nbb_ref,
                        h_ref, xpa_ref, xpb_ref, cra_ref, crb_ref, twa_ref, twb_ref, *, tps):
    tm = x_ref.shape[0]
    tf = act_ref.shape[1]
    pad = FFN_HIST_PAD
    m = pl.program_id(0)
    f = pl.program_id(1)
    first = (m % tps) == 0
    cols = [slice(c * LANES, (c + 1) * LANES) for c in range(tf // LANES)]
    halves = ((xpa_ref, wa_ref, sta_ref, cra_ref, nba_ref), (xpb_ref, wb_ref, stb_ref, crb_ref, nbb_ref))
    rb = _tile(tm, max(tm // 4, 16), 16)
    rc = _tile(rb, 64, 16)
    n_blocks = tm // rb

    def fill_history(slot):
        @pl.when(first)
        def _():
            for xp_ref, _, st_ref, _, _ in halves:
                for c, cs in enumerate(cols):
                    xp_ref[slot, c, 0:pad, :] = st_ref[0, :, cs]

        @pl.when(jnp.logical_not(first))
        def _():
            for xp_ref, _, _, cr_ref, _ in halves:
                for c, cs in enumerate(cols):
                    xp_ref[slot, c, 0:pad, :] = cr_ref[f, :, cs]

    def project(slot, i):
        rows = slice(i * rb, (i + 1) * rb)
        for xp_ref, w_ref, _, _, _ in halves:
            u = jnp.dot(h_ref[rows, :], w_ref[...], preferred_element_type=F32)
            for c, cs in enumerate(cols):
                xp_ref[slot, c, pad + i * rb:pad + (i + 1) * rb, :] = u[:, cs]

    def save_tail(slot):
        for xp_ref, _, _, cr_ref, nb_ref in halves:
            for c, cs in enumerate(cols):
                tail = xp_ref[slot, c, tm:tm + pad, :]
                cr_ref[f, :, cs] = tail
                nb_ref[0, :, cs] = tail

    def load_taps():
        for tw_ref, dw_ref, b_ref in ((twa_ref, dwa_ref, ba_ref), (twb_ref, dwb_ref, bb_ref)):
            for k in range(3):
                tw_ref[k] = jnp.broadcast_to(dw_ref[k:k + 1, :], (pad, tf))
            tw_ref[3] = jnp.broadcast_to(b_ref[...], (pad, tf))

    def conv(xp_ref, tw_ref, slot, c, r0):
        def rows_back(k):
            return xp_ref[slot, c, r0 + pad - k:r0 + pad - k + rc, :].reshape(rc // pad, pad, LANES)

        tap = lambda k: tw_ref[k, :, cols[c]]
        return tap(0) * rows_back(2) + tap(1) * rows_back(1) + tap(2) * rows_back(0) + tap(3)

    def gate(slot, i):
        for r0 in range(i * rb, (i + 1) * rb, rc):
            for c, cs in enumerate(cols):
                ca = conv(xpa_ref, twa_ref, slot, c, r0)
                cb = conv(xpb_ref, twb_ref, slot, c, r0)
                act_ref[r0:r0 + rc, cs] = (_silu(ca) * cb).reshape(rc, LANES).astype(BF16)

    @pl.when(f == 0)
    def _():
        _ffn_norm(x_ref, g_ref, h_ref)

    fill_history(0)
    load_taps()
    project(0, 0)
    for i in range(1, n_blocks):
        project(0, i)
        gate(0, i - 1)
    gate(0, n_blocks - 1)
    save_tail(0)


def _ffn_up_long(x, g, w_up, layer, w_dw, b_dw, st, *, n_seq):
    m, d = x.shape
    f2 = w_up.shape[2]
    ff = f2 // 2
    t = m // n_seq
    tm = _tile(t, ROWS_FFN, 16)
    tps = t // tm
    tf = _tile(ff, COLS_FFN, LANES)
    nf = ff // tf
    pad = FFN_HIST_PAD
    pj = gj = lambda j: j
    act, nba, nbb = pl.pallas_call(
        functools.partial(_ffn_up_long_kernel, tps=tps),
        grid=(m // tm, nf),
        in_specs=[pl.BlockSpec((tm, d), lambda i, j: (i, 0)),
                  pl.BlockSpec((1, d), lambda i, j: (0, 0)),
                  pl.BlockSpec((None, d, tf), lambda i, j: (layer, 0, pj(j))),
                  pl.BlockSpec((None, d, tf), lambda i, j: (layer, 0, pj(j) + nf)),
                  pl.BlockSpec((3, tf), lambda i, j: (0, gj(j))),
                  pl.BlockSpec((3, tf), lambda i, j: (0, gj(j) + nf)),
                  pl.BlockSpec((1, tf), lambda i, j: (0, gj(j))),
                  pl.BlockSpec((1, tf), lambda i, j: (0, gj(j) + nf)),
                  pl.BlockSpec((1, pad, tf), lambda i, j: (i // tps, 0, pj(j))),
                  pl.BlockSpec((1, pad, tf), lambda i, j: (i // tps, 0, pj(j) + nf))],
        out_specs=[pl.BlockSpec((tm, tf), lambda i, j: (i, gj(j))),
                   pl.BlockSpec((1, pad, tf), lambda i, j: (i, 0, pj(j))),
                   pl.BlockSpec((1, pad, tf), lambda i, j: (i, 0, pj(j)))],
        out_shape=[jax.ShapeDtypeStruct((m, ff), BF16),
                   jax.ShapeDtypeStruct((m // tm, pad, ff), F32),
                   jax.ShapeDtypeStruct((m // tm, pad, ff), F32)],
        scratch_shapes=[pltpu.VMEM((tm, d), BF16),
                        pltpu.VMEM((1, tf // LANES, pad + tm, LANES), F32),
                        pltpu.VMEM((1, tf // LANES, pad + tm, LANES), F32),
                        pltpu.VMEM((nf, pad, tf), F32), pltpu.VMEM((nf, pad, tf), F32),
                        pltpu.VMEM((4, pad, tf), F32), pltpu.VMEM((4, pad, tf), F32)],
        compiler_params=_params(2),
        name="ffn_up_long",
    )(x, g.reshape(1, d), w_up, w_up, w_dw, w_dw, b_dw.reshape(1, f2), b_dw.reshape(1, f2), st, st)
    last = slice(tps - 1, None, tps)
    return act, jnp.concatenate([nba[last, pad - 2:], nbb[last, pad - 2:]], axis=-1)


def _ffn_up_slab_kernel(x_ref, g_ref, wa_ref, wb_ref, dwa_ref, dwb_ref, ba_ref, bb_ref,
                        sa0_ref, sa1_ref, sb0_ref, sb1_ref,
                        act_ref, na0_ref, na1_ref, nb0_ref, nb1_ref,
                        h_ref, xpa_ref, xpb_ref, *, s):
    m = x_ref.shape[0]

    @pl.when(pl.program_id(0) == 0)
    def _():
        _ffn_norm(x_ref, g_ref, h_ref)

    def half(w_ref, dw_ref, b_ref, s0_ref, s1_ref, n0_ref, n1_ref, xp_ref):
        u = jnp.dot(h_ref[...], w_ref[...], preferred_element_type=F32)
        xp_ref[0:s, :] = s0_ref[...]
        xp_ref[s:2 * s, :] = s1_ref[...]
        xp_ref[2 * s:2 * s + m, :] = u
        n0_ref[...] = xp_ref[m:m + s, :]
        n1_ref[...] = xp_ref[m + s:m + 2 * s, :]
        return (dw_ref[0:1, :] * xp_ref[0:m, :] + dw_ref[1:2, :] * xp_ref[s:s + m, :]
                + dw_ref[2:3, :] * u + b_ref[...])

    ca = half(wa_ref, dwa_ref, ba_ref, sa0_ref, sa1_ref, na0_ref, na1_ref, xpa_ref)
    cb = half(wb_ref, dwb_ref, bb_ref, sb0_ref, sb1_ref, nb0_ref, nb1_ref, xpb_ref)
    act_ref[...] = (_silu(ca) * cb).astype(BF16)


def _ffn_up_slab(x, g, w_up, layer, w_dw, b_dw, st2d, *, s):
    m, d = x.shape
    f2 = w_up.shape[2]
    ff = f2 // 2
    tf = _tile(ff, COLS_FFN, LANES)
    nf = ff // tf
    lo = lambda j: (0, j)
    hi = lambda j: (0, j + nf)
    outs = pl.pallas_call(
        functools.partial(_ffn_up_slab_kernel, s=s),
        grid=(nf,),
        in_specs=[pl.BlockSpec((m, d), lambda j: (0, 0)),
                  pl.BlockSpec((1, d), lambda j: (0, 0)),
                  pl.BlockSpec((None, d, tf), lambda j: (layer, 0, j)),
                  pl.BlockSpec((None, d, tf), lambda j: (layer, 0, j + nf)),
                  pl.BlockSpec((3, tf), lo), pl.BlockSpec((3, tf), hi),
                  pl.BlockSpec((1, tf), lo), pl.BlockSpec((1, tf), hi),
                  pl.BlockSpec((s, tf), lo), pl.BlockSpec((s, tf), lambda j: (0, j + 2 * nf)),
                  pl.BlockSpec((s, tf), hi), pl.BlockSpec((s, tf), lambda j: (0, j + 3 * nf))],
        out_specs=[pl.BlockSpec((m, tf), lo)] + [pl.BlockSpec((s, tf), lo)] * 4,
        out_shape=[jax.ShapeDtypeStruct((m, ff), BF16)] + [jax.ShapeDtypeStruct((s, ff), F32)] * 4,
        scratch_shapes=[pltpu.VMEM((m, d), BF16),
                        pltpu.VMEM((2 * s + m, tf), F32), pltpu.VMEM((2 * s + m, tf), F32)],
        compiler_params=_params(1),
        name="ffn_up_slab",
    )(x, g.reshape(1, d), w_up, w_up, w_dw, w_dw, b_dw.reshape(1, f2), b_dw.reshape(1, f2),
      st2d, st2d, st2d, st2d)
    act, na0, na1, nb0, nb1 = outs
    new = jnp.stack([jnp.concatenate([na0, nb0], axis=-1), jnp.concatenate([na1, nb1], axis=-1)], axis=1)
    return act, new


def _front_pad(st, pad):
    return jnp.pad(st, ((0, 0), (pad - st.shape[1], 0), (0, 0)))


def _trunk(x, st_a, st_b, st_c, st_f, start, wts, *, slab):
    (norm_mix, norm_ffn, norm_final, a_w_pw1, a_w_dw, a_b_dw, a_ln_g, a_ln_b, a_w_pw2,
     b_w_grp, b_scale, c_lb, c_w_in, c_g_norm, c_w_o, f_w_up, f_w_dw, f_b_dw, f_w_down) = wts
    bsz, t, d = x.shape
    depth = norm_mix.shape[0]
    m = bsz * t
    if slab:
        to_rows = lambda a: a.transpose(1, 0, 2).reshape(m, a.shape[-1])
        from_rows = lambda a: a.reshape(t, bsz, a.shape[-1]).transpose(1, 0, 2)
    else:
        to_rows = lambda a: a.reshape(m, a.shape[-1])
        from_rows = lambda a: a.reshape(bsz, t, a.shape[-1])
    xr = to_rows(x)
    new_a, new_b, new_c, new_f = [], [], [], []
    for layer in range(depth):
        kind, j = layer % 3, layer // 3
        if kind == 0:
            v = _pro_mm(xr, a_w_pw1, j, mode="rms", p0=norm_mix[layer], glu=True, name="a_pw1")
            hist = a_w_dw.shape[1] - 1
            if slab:
                c_pre, nb = _conv_slab(v, st_a[j].reshape(bsz, -1), a_w_dw[j], a_b_dw[j], s=bsz)
                nb = nb.transpose(1, 0, 2)
            else:
                c_pre, nb = _conv_long(v, _front_pad(st_a[j], HIST_PAD), a_w_dw[j], a_b_dw[j], n_seq=bsz)
                nb = nb[:, HIST_PAD - hist:]
            new_a.append(nb)
            xr = _pro_mm(c_pre, a_w_pw2, j, mode="ln_silu", p0=a_ln_g[j], p1=a_ln_b[j], res=xr, name="a_pw2")
        elif kind == 1:
            h = _rms(xr, norm_mix[layer])
            hist = max(POOL_WINDOWS) - 1
            if slab:
                pooled, nb = _pool_slab(h, st_b[j].reshape(bsz, -1), s=bsz, start=start)
                nb = nb.transpose(1, 0, 2)
            else:
                pooled, nb = _pool_long(h, _front_pad(st_b[j], HIST_PAD), n_seq=bsz, start=start)
                nb = nb[:, HIST_PAD - hist:]
            new_b.append(nb)
            xr = _grp_mm(pooled, b_w_grp, j, b_scale[j], xr)
        else:
            parts = []
            for i, (post, arg) in enumerate((("silu", None), ("forget", c_lb), (None, None),
                                             ("silu_scale", c_g_norm[j]))):
                part = _pro_mm(xr, c_w_in, j, mode="rms", p0=norm_mix[layer], cols=(i * d, d),
                               post=post, post_arg=arg, post_layer=layer, name="c_in")
                parts.append(from_rows(part).reshape(m, d) if slab else part)
            og, s_fin = _hgrn(*parts, st_c[j], n_seq=bsz)
            if slab:
                og = to_rows(og.reshape(bsz, t, d))
            new_c.append(s_fin)
            xr = _pro_mm(og, c_w_o, j, mode="cast", res=xr, name="c_out")
        if slab:
            act, nb = _ffn_up_slab(xr, norm_ffn[layer], f_w_up, layer, f_w_dw[layer], f_b_dw[layer],
                                   st_f[layer].reshape(bsz, -1), s=bsz)
        else:
            act, nb = _ffn_up_long(xr, norm_ffn[layer], f_w_up, layer, f_w_dw[layer], f_b_dw[layer],
                                   _front_pad(st_f[layer], FFN_HIST_PAD), n_seq=bsz)
        new_f.append(nb)
        xr = _pro_mm(act, f_w_down, layer, mode="cast", res=xr, name="f_down")
    y = from_rows(_rms(xr, norm_final))
    return y, jnp.stack(new_a), jnp.stack(new_b), jnp.stack(new_c), jnp.stack(new_f)


def kernel(x_prompt, x_sample, state_conv_a, state_pool, state_hgrn, state_ffn_conv, norm_mix, norm_ffn, norm_final, a_w_pw1, a_w_dw, a_b_dw, a_ln_g, a_ln_b, a_w_pw2, b_w_grp, b_scale, c_lb, c_w_in, c_g_norm, c_w_o, f_w_up, f_w_dw, f_b_dw, f_w_down):
    bf = lambda w: w.astype(BF16)
    wts = (norm_mix, norm_ffn, norm_final, bf(a_w_pw1), a_w_dw, a_b_dw, a_ln_g, a_ln_b, bf(a_w_pw2),
           bf(b_w_grp), b_scale, c_lb, bf(c_w_in), c_g_norm, bf(c_w_o), bf(f_w_up), f_w_dw, f_b_dw,
           bf(f_w_down))
    bp = x_prompt.shape[0]
    zeros = lambda st: jnp.zeros((st.shape[0], bp) + st.shape[2:], st.dtype)
    yp, pa, pb, pc, pf = _trunk(x_prompt, zeros(state_conv_a), zeros(state_pool), zeros(state_hgrn),
                                zeros(state_ffn_conv), 0, wts, slab=False)
    ys, sa, sb, sc, sf = _trunk(x_sample, state_conv_a, state_pool, state_hgrn, state_ffn_conv,
                                PAST_LEN, wts, slab=True)
    return (yp, ys, pa, sa, pb, sb, pc, sc, pf, sf)
```

```python
import functools

import numpy as np
import jax
import jax.numpy as jnp
from jax import lax
from jax.experimental import pallas as pl
from jax.experimental.pallas import tpu as pltpu

F32 = jnp.float32
BF16 = jnp.bfloat16

NORM_EPS = 1e-6
POOL_WINDOWS = (2, 4, 8, 16)
HGRN_DK = 128
LANES = 128
SUBLANES = 8
HIST_PAD = 32
FFN_HIST_PAD = 8
VMEM_LIMIT_BYTES = 56 * 1024 * 1024
PAST_LEN = 16384
ROWS_MM = 1024
COLS_MM = 1024
COLS_FFN = 512
MM_VMEM_BUDGET = 46 * 1024 * 1024
ROWS_CONV = 512
COLS_CONV = 512
ROWS_FFN = 1024


def _params(n_axes):
    return pltpu.CompilerParams(dimension_semantics=("arbitrary",) * n_axes,
                                vmem_limit_bytes=VMEM_LIMIT_BYTES)


def _tile(n, pref, mult):
    t = min(pref, n)
    t -= t % mult
    while t >= mult:
        if n % t == 0:
            return t
        t -= mult
    return n


def _sigmoid(x):
    return 1.0 / (1.0 + jnp.exp(-x))


def _silu(x):
    return x * _sigmoid(x)


def _rms_rows(x, g):
    ms = jnp.mean(x * x, axis=-1, keepdims=True)
    return x * lax.rsqrt(ms + NORM_EPS) * g


def _ln_silu_rows(x, g, b):
    mu = jnp.mean(x, axis=-1, keepdims=True)
    d = x - mu
    var = jnp.mean(d * d, axis=-1, keepdims=True)
    return _silu(d * lax.rsqrt(var + NORM_EPS) * g + b)


def _row_chunks(tm, fn):
    rc = _tile(tm, 128, SUBLANES)

    def body(i, carry):
        fn(pl.ds(pl.multiple_of(i * rc, rc), rc))
        return carry

    lax.fori_loop(0, tm // rc, body, 0)


def _rms_kernel(x_ref, g_ref, o_ref):
    tm = x_ref.shape[0]

    def chunk(rows):
        o_ref[rows, :] = _rms_rows(x_ref[rows, :], g_ref[...]).astype(o_ref.dtype)

    _row_chunks(tm, chunk)


def _rms(x, g, out_dtype=F32):
    m, d = x.shape
    tm = _tile(m, 512, 16)
    return pl.pallas_call(
        _rms_kernel,
        grid=(m // tm,),
        in_specs=[pl.BlockSpec((tm, d), lambda i: (i, 0)),
                  pl.BlockSpec((1, d), lambda i: (0, 0))],
        out_specs=pl.BlockSpec((tm, d), lambda i: (i, 0)),
        out_shape=jax.ShapeDtypeStruct((m, d), out_dtype),
        compiler_params=_params(1),
        name="rms",
    )(x, g.reshape(1, d))


def _forget_floor(clb, layer):
    e = jnp.exp(clb - jnp.max(clb, axis=0, keepdims=True))
    sm = e / jnp.sum(e, axis=0, keepdims=True)
    lb = jnp.zeros_like(sm[0:1])
    for i in range(1, layer + 1):
        lb = lb + sm[i:i + 1]
    return lb


def _pro_mm_kernel(*refs, mode, glu, has_res, post, post_layer):
    it = iter(refs)
    x_ref = next(it)
    p0_ref = next(it) if mode in ("rms", "ln_silu") else None
    p1_ref = next(it) if mode == "ln_silu" else None
    w_ref = next(it)
    wg_ref = next(it) if glu else None
    res_ref = next(it) if has_res else None
    e_ref = next(it) if post in ("forget", "silu_scale") else None
    o_ref = next(it)
    tm = x_ref.shape[0]

    a_ref = x_ref if x_ref.dtype == BF16 else next(it)
    rb = tm if (a_ref is x_ref and post is None) else _tile(tm, max(tm // 4, 16), 16)

    def prologue(rows0):
        step = _tile(rb, 32, 16)
        for r0 in range(rows0, rows0 + rb, step):
            xv = x_ref[r0:r0 + step, :]
            if mode == "rms":
                xv = _rms_rows(xv, p0_ref[...])
            elif mode == "ln_silu":
                xv = _ln_silu_rows(xv, p0_ref[...], p1_ref[...])
            a_ref[r0:r0 + step, :] = xv.astype(BF16)

    def product(rows0):
        rows = slice(rows0, rows0 + rb)
        a = a_ref[rows, :]
        acc = jnp.dot(a, w_ref[...], preferred_element_type=F32)
        if glu:
            acc = acc * _sigmoid(jnp.dot(a, wg_ref[...], preferred_element_type=F32))
        if has_res:
            acc = res_ref[rows, :] + acc
        if post == "silu":
            acc = _silu(acc)
        elif post == "silu_scale":
            acc = _silu(acc) * e_ref[...]
        elif post == "forget":
            lb = _forget_floor(e_ref[...], post_layer)
            acc = lb + (1.0 - lb) * _sigmoid(acc)
        o_ref[rows, :] = acc

    if a_ref is x_ref:
        for rows0 in range(0, tm, rb):
            product(rows0)
    else:
        @pl.when(pl.program_id(1) == 0)
        def _():
            for rows0 in range(0, tm, rb):
                prologue(rows0)
                product(rows0)

        @pl.when(pl.program_id(1) > 0)
        def _():
            for rows0 in range(0, tm, rb):
                product(rows0)


def _mm_vmem_bytes(tm, k, tn, x_bytes, glu, has_res):
    n_w = 2 if glu else 1
    blocks = 2 * (tm * k * x_bytes + n_w * k * tn * 2 + (2 if has_res else 1) * tm * tn * 4)
    return blocks + (tm * k * 2 if x_bytes == 4 else 0) + (n_w + 1) * tm * tn * 4


def _pro_mm(x, w, layer, *, mode, p0=None, p1=None, glu=False, res=None, cols=None,
            post=None, post_arg=None, post_layer=0, name):
    m, k = x.shape
    c0, n = cols if cols is not None else (0, w.shape[2] // (2 if glu else 1))
    tm = _tile(m, ROWS_MM, 16)
    tn = _tile(n, COLS_MM, LANES)
    while tn > LANES and _mm_vmem_bytes(tm, k, tn, x.dtype.itemsize, glu, res is not None) > MM_VMEM_BUDGET:
        tn = _tile(n, tn // 2, LANES)
    nn = n // tn
    args = [x]
    in_specs = [pl.BlockSpec((tm, k), lambda i, j: (i, 0))]
    for p in (p0, p1):
        if p is not None:
            args.append(p.reshape(1, k))
            in_specs.append(pl.BlockSpec((1, k), lambda i, j: (0, 0)))
    assert c0 % tn == 0
    j0 = c0 // tn
    args.append(w)
    in_specs.append(pl.BlockSpec((None, k, tn), lambda i, j: (layer, 0, j0 + j)))
    if glu:
        args.append(w)
        in_specs.append(pl.BlockSpec((None, k, tn), lambda i, j: (layer, 0, j0 + j + nn)))
    if res is not None:
        args.append(res)
        in_specs.append(pl.BlockSpec((tm, tn), lambda i, j: (i, j)))
    if post == "silu_scale":
        args.append(post_arg.reshape(1, n))
        in_specs.append(pl.BlockSpec((1, tn), lambda i, j: (0, j)))
    elif post == "forget":
        args.append(post_arg)
        in_specs.append(pl.BlockSpec((post_arg.shape[0], tn), lambda i, j: (0, j)))
    return pl.pallas_call(
        functools.partial(_pro_mm_kernel, mode=mode, glu=glu, has_res=res is not None,
                          post=post, post_layer=post_layer),
        grid=(m // tm, nn),
        in_specs=in_specs,
        out_specs=pl.BlockSpec((tm, tn), lambda i, j: (i, j)),
        out_shape=jax.ShapeDtypeStruct((m, n), F32),
        scratch_shapes=[] if x.dtype == BF16 else [pltpu.VMEM((tm, k), BF16)],
        compiler_params=_params(2),
        name=name,
    )(*args)


def _conv_long_kernel(v_ref, st_ref, w_ref, b_ref, c_ref, nb_ref, xp_ref, wk_ref, cs_ref, *, tps):
    tm, cb = v_ref.shape
    width = w_ref.shape[0]
    off = HIST_PAD - (width - 1)
    rc = _tile(tm, 128, SUBLANES)
    m = pl.program_id(1)
    start = (m % tps) == 0
    cols = [slice(j * LANES, (j + 1) * LANES) for j in range(cb // LANES)]

    @pl.when(start)
    def _():
        for j, cs in enumerate(cols):
            xp_ref[j, 0:HIST_PAD, :] = st_ref[0, :, cs]

    @pl.when(jnp.logical_not(start))
    def _():
        for j in range(len(cols)):
            xp_ref[j, 0:HIST_PAD, :] = xp_ref[j, tm:tm + HIST_PAD, :]

    for j, cs in enumerate(cols):
        xp_ref[j, HIST_PAD:HIST_PAD + tm, :] = v_ref[:, cs]
        wk_ref[j, 0:width, :] = w_ref[:, cs]
        wk_ref[j, width:width + 1, :] = b_ref[:, cs]

    def column(j, carry):
        for r in range(tm // rc):
            acc = jnp.broadcast_to(wk_ref[j, width:width + 1, :], (rc, LANES))
            for rho in range(min(SUBLANES, width)):
                taps = range(rho, width, SUBLANES)
                lo = r * rc + off + rho
                win = xp_ref[j, lo:lo + rc + SUBLANES * (len(taps) - 1), :]
                for i, k in enumerate(taps):
                    acc = acc + wk_ref[j, k:k + 1, :] * win[SUBLANES * i:SUBLANES * i + rc]
            cs_ref[j, r * rc:(r + 1) * rc, :] = acc
        return carry

    lax.fori_loop(0, len(cols), column, 0)
    for j, cs in enumerate(cols):
        c_ref[:, cs] = cs_ref[j]
        nb_ref[0, :, cs] = xp_ref[j, tm:tm + HIST_PAD, :]


def _conv_long(v, st, w_dw, b_dw, *, n_seq):
    m, c = v.shape
    width = w_dw.shape[0]
    t = m // n_seq
    tm = _tile(t, ROWS_CONV, SUBLANES)
    assert tm >= HIST_PAD
    tps = t // tm
    cb = _tile(c, COLS_CONV, LANES)
    ncl = cb // LANES
    return pl.pallas_call(
        functools.partial(_conv_long_kernel, tps=tps),
        grid=(c // cb, m // tm),
        in_specs=[pl.BlockSpec((tm, cb), lambda j, i: (i, j)),
                  pl.BlockSpec((1, HIST_PAD, cb), lambda j, i: (i // tps, 0, j)),
                  pl.BlockSpec((width, cb), lambda j, i: (0, j)),
                  pl.BlockSpec((1, cb), lambda j, i: (0, j))],
        out_specs=[pl.BlockSpec((tm, cb), lambda j, i: (i, j)),
                   pl.BlockSpec((1, HIST_PAD, cb), lambda j, i: (i // tps, 0, j))],
        out_shape=[jax.ShapeDtypeStruct((m, c), F32),
                   jax.ShapeDtypeStruct((n_seq, HIST_PAD, c), F32)],
        scratch_shapes=[pltpu.VMEM((ncl, HIST_PAD + tm, LANES), F32),
                        pltpu.VMEM((ncl, width + 1, LANES), F32),
                        pltpu.VMEM((ncl, tm, LANES), F32)],
        compiler_params=_params(2),
        name="conv_long",
    )(v, st, w_dw, b_dw.reshape(1, c))


def _conv_slab_kernel(*refs, hist, n_tok, s):
    v_ref = refs[0]
    st_refs = refs[1:1 + hist]
    w_ref, b_ref, c_ref, nb_ref = refs[1 + hist:]
    width = w_ref.shape[0]

    def slab(i):
        if i < hist:
            return st_refs[i][...]
        return v_ref[(i - hist) * s:(i - hist + 1) * s, :]

    for t in range(n_tok):
        acc = jnp.broadcast_to(b_ref[...], (s, LANES))
        for k in range(width):
            acc = acc + w_ref[k:k + 1, :] * slab(t + k)
        c_ref[t * s:(t + 1) * s, :] = acc
    for j in range(hist):
        nb_ref[j] = slab(j + n_tok)


def _conv_slab(v, st2d, w_dw, b_dw, *, s):
    m, c = v.shape
    width = w_dw.shape[0]
    hist = width - 1
    n_tok = m // s
    ncb = c // LANES
    in_specs = [pl.BlockSpec((m, LANES), lambda j: (0, j))]
    in_specs += [pl.BlockSpec((s, LANES), lambda j, jj=jj: (0, jj * ncb + j)) for jj in range(hist)]
    in_specs += [pl.BlockSpec((width, LANES), lambda j: (0, j)),
                 pl.BlockSpec((1, LANES), lambda j: (0, j))]
    return pl.pallas_call(
        functools.partial(_conv_slab_kernel, hist=hist, n_tok=n_tok, s=s),
        grid=(ncb,),
        in_specs=in_specs,
        out_specs=[pl.BlockSpec((m, LANES), lambda j: (0, j)),
                   pl.BlockSpec((hist, s, LANES), lambda j: (0, 0, j))],
        out_shape=[jax.ShapeDtypeStruct((m, c), F32),
                   jax.ShapeDtypeStruct((hist, s, c), F32)],
        compiler_params=_params(1),
        name="conv_slab",
    )(v, *([st2d] * hist), w_dw, b_dw.reshape(1, c))


def _select_by_group(gid, vals):
    out = vals[-1]
    for g in range(len(vals) - 2, -1, -1):
        out = jnp.where(gid == g, vals[g], out)
    return out


def _pool_long_kernel(h_ref, st_ref, p_ref, nb_ref, xp_ref, s1_ref, s2_ref, s3_ref, s4_ref,
                      *, tps, start, lanes_per_group):
    tm, cb = h_ref.shape
    n = HIST_PAD + tm
    m = pl.program_id(1)
    first = (m % tps) == 0
    cols = [slice(j * LANES, (j + 1) * LANES) for j in range(cb // LANES)]

    @pl.when(first)
    def _():
        for j, cs in enumerate(cols):
            xp_ref[j, 0:HIST_PAD, :] = st_ref[0, :, cs]

    @pl.when(jnp.logical_not(first))
    def _():
        for j in range(len(cols)):
            xp_ref[j, 0:HIST_PAD, :] = xp_ref[j, tm:tm + HIST_PAD, :]

    for j, cs in enumerate(cols):
        gid = (pl.program_id(0) * len(cols) + j) // lanes_per_group
        xp_ref[j, HIST_PAD:n, :] = h_ref[:, cs]
        s1_ref[8:n, :] = xp_ref[j, 8:n, :] + xp_ref[j, 7:n - 1, :]
        s2_ref[16:n, :] = s1_ref[16:n, :] + s1_ref[14:n - 2, :]
        s3_ref[24:n, :] = s2_ref[24:n, :] + s2_ref[20:n - 4, :]
        s4_ref[32:n, :] = s3_ref[32:n, :] + s3_ref[24:n - 8, :]
        sums = [r[HIST_PAD:n, :] for r in (s1_ref, s2_ref, s3_ref, s4_ref)]
        if start >= max(POOL_WINDOWS) - 1:
            means = [sm * (1.0 / w) for sm, w in zip(sums, POOL_WINDOWS)]
        else:
            pos = start + (m % tps) * tm + lax.broadcasted_iota(jnp.int32, (tm, LANES), 0)
            means = [sm / jnp.minimum(w, pos + 1).astype(F32) for sm, w in zip(sums, POOL_WINDOWS)]
        p_ref[:, cs] = (_select_by_group(gid, means) - h_ref[:, cs]).astype(p_ref.dtype)
        nb_ref[0, :, cs] = xp_ref[j, tm:tm + HIST_PAD, :]


def _pool_long(h, st, *, n_seq, start):
    m, d = h.shape
    t = m // n_seq
    tm = _tile(t, ROWS_CONV, 16)
    assert tm >= HIST_PAD
    tps = t // tm
    lanes_per_group = d // len(POOL_WINDOWS) // LANES
    cb = _tile(d, COLS_CONV, LANES)
    return pl.pallas_call(
        functools.partial(_pool_long_kernel, tps=tps, start=start, lanes_per_group=lanes_per_group),
        grid=(d // cb, m // tm),
        in_specs=[pl.BlockSpec((tm, cb), lambda j, i: (i, j)),
                  pl.BlockSpec((1, HIST_PAD, cb), lambda j, i: (i // tps, 0, j))],
        out_specs=[pl.BlockSpec((tm, cb), lambda j, i: (i, j)),
                   pl.BlockSpec((1, HIST_PAD, cb), lambda j, i: (i // tps, 0, j))],
        out_shape=[jax.ShapeDtypeStruct((m, d), BF16),
                   jax.ShapeDtypeStruct((n_seq, HIST_PAD, d), F32)],
        scratch_shapes=[pltpu.VMEM((cb // LANES, HIST_PAD + tm, LANES), F32)]
        + [pltpu.VMEM((HIST_PAD + tm, LANES), F32) for _ in range(4)],
        compiler_params=_params(2),
        name="pool_long",
    )(h, st)


def _pool_slab_kernel(*refs, hist, n_tok, s, start, lanes_per_group):
    h_ref = refs[0]
    st_refs = refs[1:1 + hist]
    p_ref, nb_ref = refs[1 + hist:]
    gid = pl.program_id(0) // lanes_per_group

    def slab(i):
        if i < hist:
            return st_refs[i][...]
        return h_ref[(i - hist) * s:(i - hist + 1) * s, :]

    n = hist + n_tok
    level = [slab(i) for i in range(n)]
    levels = []
    step = 1
    for _ in POOL_WINDOWS:
        level = [level[i] + level[i - step] if i >= 2 * step - 1 else None for i in range(n)]
        levels.append(level)
        step *= 2
    for t in range(n_tok):
        means = [lv[hist + t] * (1.0 / min(w, start + t + 1)) for lv, w in zip(levels, POOL_WINDOWS)]
        p_ref[t * s:(t + 1) * s, :] = (_select_by_group(gid, means) - slab(hist + t)).astype(p_ref.dtype)
    for j in range(hist):
        nb_ref[j] = slab(j + n_tok)


def _pool_slab(h, st2d, *, s, start):
    m, d = h.shape
    hist = max(POOL_WINDOWS) - 1
    n_tok = m // s
    ncb = d // LANES
    lanes_per_group = d // len(POOL_WINDOWS) // LANES
    in_specs = [pl.BlockSpec((m, LANES), lambda j: (0, j))]
    in_specs += [pl.BlockSpec((s, LANES), lambda j, jj=jj: (0, jj * ncb + j)) for jj in range(hist)]
    return pl.pallas_call(
        functools.partial(_pool_slab_kernel, hist=hist, n_tok=n_tok, s=s, start=start,
                          lanes_per_group=lanes_per_group),
        grid=(ncb,),
        in_specs=in_specs,
        out_specs=[pl.BlockSpec((m, LANES), lambda j: (0, j)),
                   pl.BlockSpec((hist, s, LANES), lambda j: (0, 0, j))],
        out_shape=[jax.ShapeDtypeStruct((m, d), BF16),
                   jax.ShapeDtypeStruct((hist, s, d), F32)],
        compiler_params=_params(1),
        name="pool_slab",
    )(h, *([st2d] * hist))


def _grp_mm_kernel(p_ref, w_ref, sc_ref, res_ref, o_ref):
    y = jnp.dot(p_ref[...], w_ref[0], preferred_element_type=F32)
    o_ref[...] = res_ref[...] + y * sc_ref[...]


def _grp_mm(p, w_grp, layer, scale, res):
    m, d = p.shape
    _, ng, gc, _ = w_grp.shape
    tm = _tile(m, ROWS_MM, 16)
    return pl.pallas_call(
        _grp_mm_kernel,
        grid=(m // tm, ng),
        in_specs=[pl.BlockSpec((tm, gc), lambda i, g: (i, g)),
                  pl.BlockSpec((None, 1, gc, gc), lambda i, g: (layer, g, 0, 0)),
                  pl.BlockSpec((1, gc), lambda i, g: (0, g)),
                  pl.BlockSpec((tm, gc), lambda i, g: (i, g))],
        out_specs=pl.BlockSpec((tm, gc), lambda i, g: (i, g)),
        out_shape=jax.ShapeDtypeStruct((m, d), F32),
        compiler_params=_params(2),
        name="grp_mm",
    )(p, w_grp, scale.reshape(1, d), res)


def _cumsum_rows(x):
    n = x.shape[0]
    row = lax.broadcasted_iota(jnp.int32, x.shape, 0)
    d = 1
    while d < n:
        x = x + jnp.where(row >= d, pltpu.roll(x, d, axis=0), 0.0)
        d *= 2
    return x


def _level_table(c):
    t = np.arange(c)[:, None]
    s = np.arange(c)[None, :]
    x = np.maximum(t ^ s, 1)
    top = np.left_shift(1, np.floor(np.log2(x)).astype(np.int64))
    return np.where(s < t, top, np.where(s == t, 0, -1)).astype(np.int32)


def _level_ref(g_cum, length):
    c = g_cum.shape[0]
    if length >= SUBLANES:
        parts = [jnp.broadcast_to(g_cum[b + length - 1:b + length, :], (2 * length, HGRN_DK))
                 for b in range(0, c, 2 * length)]
        return parts[0] if len(parts) == 1 else jnp.concatenate(parts, axis=0)
    g3 = g_cum.reshape(c // SUBLANES, SUBLANES, HGRN_DK)
    sub = lax.broadcasted_iota(jnp.int32, g3.shape, 1)
    ref = None
    for b in range(SUBLANES - 2 * length, -1, -2 * length):
        cand = jnp.broadcast_to(g3[:, b + length - 1:b + length, :], g3.shape)
        ref = cand if ref is None else jnp.where(sub < b + 2 * length, cand, ref)
    return ref.reshape(c, HGRN_DK)


def _tile_pairs(q, kk, g_cum, inp):
    c = q.shape[0]
    t_idx = lax.broadcasted_iota(jnp.int32, (c, HGRN_DK), 0)
    ps = []
    for s in range(c):
        d = jnp.where(t_idx >= s, g_cum - g_cum[s:s + 1, :], -jnp.inf)
        ps.append(q * jnp.exp(d) * kk[s:s + 1, :])
    a_all = jnp.dot(jnp.concatenate(ps, axis=0), jnp.ones((HGRN_DK, LANES), F32),
                    preferred_element_type=F32)
    out = jnp.zeros((c, HGRN_DK), F32)
    for s in range(c):
        out = out + a_all[s * c:(s + 1) * c] * inp[s:s + 1, :]
    return out


def _hgrn_chunk(q, f, inp, s0, lv):
    c = q.shape[0]
    kk = 1.0 - f
    g_cum = _cumsum_rows(jnp.log(f))
    o = jnp.dot(q * jnp.exp(g_cum), s0, preferred_element_type=F32)
    if c == SUBLANES:
        o = o + _tile_pairs(q, kk, g_cum, inp)
    else:
        scores = jnp.zeros((c, c), F32)
        length = 0
        while length < c:
            if length == 0:
                qt, kt = q, kk
            else:
                e = jnp.exp(-jnp.abs(g_cum - _level_ref(g_cum, length)))
                qt, kt = q * e, kk * e
            sc = lax.dot_general(qt, kt, (((1,), (1,)), ((), ())), preferred_element_type=F32)
            scores = jnp.where(lv == length, sc, scores)
            length = max(1, 2 * length)
        o = o + jnp.dot(scores, inp, preferred_element_type=F32)
    g_end = g_cum[c - 1:c, :]
    kt = kk * jnp.exp(g_end - g_cum)
    upd = lax.dot_general(kt, inp, (((0,), (0,)), ((), ())), preferred_element_type=F32)
    decay = jnp.transpose(jnp.broadcast_to(jnp.exp(g_end), (HGRN_DK, HGRN_DK)))
    return o, decay * s0 + upd


def _hgrn_kernel(q_ref, f_ref, i_ref, g_ref, lv_ref, s0_ref, o_ref, sf_ref, *, hu, chunk):
    @pl.when(pl.program_id(2) == 0)
    def _():
        sf_ref[...] = s0_ref[...]

    sb = sf_ref.shape[0]
    tb = q_ref.shape[0] // sb
    lv = lv_ref[...]
    for si in range(sb):
        for h in range(hu):
            hs = slice(h * HGRN_DK, (h + 1) * HGRN_DK)
            state = sf_ref[si, h]
            for cix in range(tb // chunk):
                rows = slice(si * tb + cix * chunk, si * tb + (cix + 1) * chunk)
                o, state = _hgrn_chunk(q_ref[rows, hs], f_ref[rows, hs], i_ref[rows, hs], state, lv)
                o = o * lax.rsqrt(jnp.mean(o * o, axis=-1, keepdims=True) + NORM_EPS)
                o_ref[rows, hs] = o * g_ref[rows, hs]
            sf_ref[si, h] = state


def _hgrn(q, f, inp, gate, s0, *, n_seq):
    m, d = q.shape
    nh = d // HGRN_DK
    t = m // n_seq
    chunk = _tile(t, 128, SUBLANES)
    tb = _tile(t, 2 * chunk, chunk)
    short = t == tb == chunk == SUBLANES
    hu = nh if short else min(4, nh)
    sb = _tile(n_seq, 4, 1) if short else 1
    nhg = nh // hu
    ntb = t // tb
    wblk = hu * HGRN_DK
    rblk = sb * tb
    rows = pl.BlockSpec((rblk, wblk), lambda b, hg, c: (b * ntb + c, hg))
    return pl.pallas_call(
        functools.partial(_hgrn_kernel, hu=hu, chunk=chunk),
        grid=(n_seq // sb, nhg, ntb),
        in_specs=[rows, rows, rows, rows,
                  pl.BlockSpec((chunk, chunk), lambda b, hg, c: (0, 0)),
                  pl.BlockSpec((sb, hu, HGRN_DK, HGRN_DK), lambda b, hg, c: (b, hg, 0, 0))],
        out_specs=[pl.BlockSpec((rblk, wblk), lambda b, hg, c: (b * ntb + c, hg)),
                   pl.BlockSpec((sb, hu, HGRN_DK, HGRN_DK), lambda b, hg, c: (b, hg, 0, 0))],
        out_shape=[jax.ShapeDtypeStruct((m, d), F32),
                   jax.ShapeDtypeStruct(s0.shape, F32)],
        compiler_params=_params(3),
        name="hgrn",
    )(q, f, inp, gate, jnp.asarray(_level_table(chunk)), s0)


def _ffn_norm(x_ref, g_ref, h_ref):
    def chunk(rows):
        h_ref[rows, :] = _rms_rows(x_ref[rows, :], g_ref[...]).astype(BF16)

    _row_chunks(x_ref.shape[0], chunk)


def _ffn_up_long_kernel(x_ref, g_ref, wa_ref, wb_ref, dwa_ref, dwb_ref, ba_ref, bb_ref,
                        sta_ref, stb_ref, act_ref, nba_ref, nbb_ref,
                        h_ref, xpa_ref, xpb_ref, cra_ref, crb_ref, twa_ref, twb_ref, *, tps):
    tm = x_ref.shape[0]
    tf = act_ref.shape[1]
    pad = FFN_HIST_PAD
    m = pl.program_id(0)
    f = pl.program_id(1)
    first = (m % tps) == 0
    cols = [slice(c * LANES, (c + 1) * LANES) for c in range(tf // LANES)]
    halves = ((xpa_ref, wa_ref, sta_ref, cra_ref, nba_ref), (xpb_ref, wb_ref, stb_ref, crb_ref, nbb_ref))
    rb = _tile(tm, max(tm // 4, 16), 16)
    rc = _tile(rb, 64, 16)
    n_blocks = tm // rb

    def fill_history(slot):
        @pl.when(first)
        def _():
            for xp_ref, _, st_ref, _, _ in halves:
                for c, cs in enumerate(cols):
                    xp_ref[slot, c, 0:pad, :] = st_ref[0, :, cs]

        @pl.when(jnp.logical_not(first))
        def _():
            for xp_ref, _, _, cr_ref, _ in halves:
                for c, cs in enumerate(cols):
                    xp_ref[slot, c, 0:pad, :] = cr_ref[f, :, cs]

    def project(slot, i):
        rows = slice(i * rb, (i + 1) * rb)
        for xp_ref, w_ref, _, _, _ in halves:
            u = jnp.dot(h_ref[rows, :], w_ref[...], preferred_element_type=F32)
            for c, cs in enumerate(cols):
                xp_ref[slot, c, pad + i * rb:pad + (i + 1) * rb, :] = u[:, cs]

    def save_tail(slot):
        for xp_ref, _, _, cr_ref, nb_ref in halves:
            for c, cs in enumerate(cols):
                tail = xp_ref[slot, c, tm:tm + pad, :]
                cr_ref[f, :, cs] = tail
                nb_ref[0, :, cs] = tail

    def load_taps():
        for tw_ref, dw_ref, b_ref in ((twa_ref, dwa_ref, ba_ref), (twb_ref, dwb_ref, bb_ref)):
            for k in range(3):
                tw_ref[k] = jnp.broadcast_to(dw_ref[k:k + 1, :], (pad, tf))
            tw_ref[3] = jnp.broadcast_to(b_ref[...], (pad, tf))

    def conv(xp_ref, tw_ref, slot, c, r0):
        def rows_back(k):
            return xp_ref[slot, c, r0 + pad - k:r0 + pad - k + rc, :].reshape(rc // pad, pad, LANES)

        tap = lambda k: tw_ref[k, :, cols[c]]
        return tap(0) * rows_back(2) + tap(1) * rows_back(1) + tap(2) * rows_back(0) + tap(3)

    def gate(slot, i):
        for r0 in range(i * rb, (i + 1) * rb, rc):
            for c, cs in enumerate(cols):
                ca = conv(xpa_ref, twa_ref, slot, c, r0)
                cb = conv(xpb_ref, twb_ref, slot, c, r0)
                act_ref[r0:r0 + rc, cs] = (_silu(ca) * cb).reshape(rc, LANES).astype(BF16)

    @pl.when(f == 0)
    def _():
        _ffn_norm(x_ref, g_ref, h_ref)

    fill_history(0)
    load_taps()
    project(0, 0)
    for i in range(1, n_blocks):
        project(0, i)
        gate(0, i - 1)
    gate(0, n_blocks - 1)
    save_tail(0)


def _ffn_up_long(x, g, w_up, layer, w_dw, b_dw, st, *, n_seq):
    m, d = x.shape
    f2 = w_up.shape[2]
    ff = f2 // 2
    t = m // n_seq
    tm = _tile(t, ROWS_FFN, 16)
    tps = t // tm
    tf = _tile(ff, COLS_FFN, LANES)
    nf = ff // tf
    pad = FFN_HIST_PAD
    pj = gj = lambda j: j
    act, nba, nbb = pl.pallas_call(
        functools.partial(_ffn_up_long_kernel, tps=tps),
        grid=(m // tm, nf),
        in_specs=[pl.BlockSpec((tm, d), lambda i, j: (i, 0)),
                  pl.BlockSpec((1, d), lambda i, j: (0, 0)),
                  pl.BlockSpec((None, d, tf), lambda i, j: (layer, 0, pj(j))),
                  pl.BlockSpec((None, d, tf), lambda i, j: (layer, 0, pj(j) + nf)),
                  pl.BlockSpec((3, tf), lambda i, j: (0, gj(j))),
                  pl.BlockSpec((3, tf), lambda i, j: (0, gj(j) + nf)),
                  pl.BlockSpec((1, tf), lambda i, j: (0, gj(j))),
                  pl.BlockSpec((1, tf), lambda i, j: (0, gj(j) + nf)),
                  pl.BlockSpec((1, pad, tf), lambda i, j: (i // tps, 0, pj(j))),
                  pl.BlockSpec((1, pad, tf), lambda i, j: (i // tps, 0, pj(j) + nf))],
        out_specs=[pl.BlockSpec((tm, tf), lambda i, j: (i, gj(j))),
                   pl.BlockSpec((1, pad, tf), lambda i, j: (i, 0, pj(j))),
                   pl.BlockSpec((1, pad, tf), lambda i, j: (i, 0, pj(j)))],
        out_shape=[jax.ShapeDtypeStruct((m, ff), BF16),
                   jax.ShapeDtypeStruct((m // tm, pad, ff), F32),
                   jax.ShapeDtypeStruct((m // tm, pad, ff), F32)],
        scratch_shapes=[pltpu.VMEM((tm, d), BF16),
                        pltpu.VMEM((1, tf // LANES, pad + tm, LANES), F32),
                        pltpu.VMEM((1, tf // LANES, pad + tm, LANES), F32),
                        pltpu.VMEM((nf, pad, tf), F32), pltpu.VMEM((nf, pad, tf), F32),
                        pltpu.VMEM((4, pad, tf), F32), pltpu.VMEM((4, pad, tf), F32)],
        compiler_params=_params(2),
        name="ffn_up_long",
    )(x, g.reshape(1, d), w_up, w_up, w_dw, w_dw, b_dw.reshape(1, f2), b_dw.reshape(1, f2), st, st)
    last = slice(tps - 1, None, tps)
    return act, jnp.concatenate([nba[last, pad - 2:], nbb[last, pad - 2:]], axis=-1)


def _ffn_up_slab_kernel(x_ref, g_ref, wa_ref, wb_ref, dwa_ref, dwb_ref, ba_ref, bb_ref,
                        sa0_ref, sa1_ref, sb0_ref, sb1_ref, act_ref, new_ref,
                        h_ref, xpa_ref, xpb_ref, *, s):
    m = x_ref.shape[0]

    @pl.when(pl.program_id(0) == 0)
    def _():
        _ffn_norm(x_ref, g_ref, h_ref)

    def half(w_ref, dw_ref, b_ref, s0_ref, s1_ref, idx, xp_ref):
        u = jnp.dot(h_ref[...], w_ref[...], preferred_element_type=F32)
        xp_ref[0:s, :] = s0_ref[...]
        xp_ref[s:2 * s, :] = s1_ref[...]
        xp_ref[2 * s:2 * s + m, :] = u
        new_ref[idx] = xp_ref[m:m + s, :]
        new_ref[2 + idx] = xp_ref[m + s:m + 2 * s, :]
        return (dw_ref[0:1, :] * xp_ref[0:m, :] + dw_ref[1:2, :] * xp_ref[s:s + m, :]
                + dw_ref[2:3, :] * u + b_ref[...])

    ca = half(wa_ref, dwa_ref, ba_ref, sa0_ref, sa1_ref, 0, xpa_ref)
    cb = half(wb_ref, dwb_ref, bb_ref, sb0_ref, sb1_ref, 1, xpb_ref)
    act_ref[...] = (_silu(ca) * cb).astype(BF16)


def _ffn_up_slab(x, g, w_up, layer, w_dw, b_dw, st2d, *, s):
    m, d = x.shape
    f2 = w_up.shape[2]
    ff = f2 // 2
    tf = _tile(ff, COLS_FFN, LANES)
    nf = ff // tf
    lo = lambda j: (0, j)
    hi = lambda j: (0, j + nf)
    outs = pl.pallas_call(
        functools.partial(_ffn_up_slab_kernel, s=s),
        grid=(nf,),
        in_specs=[pl.BlockSpec((m, d), lambda j: (0, 0)),
                  pl.BlockSpec((1, d), lambda j: (0, 0)),
                  pl.BlockSpec((None, d, tf), lambda j: (layer, 0, j)),
                  pl.BlockSpec((None, d, tf), lambda j: (layer, 0, j + nf)),
                  pl.BlockSpec((3, tf), lo), pl.BlockSpec((3, tf), hi),
                  pl.BlockSpec((1, tf), lo), pl.BlockSpec((1, tf), hi),
                  pl.BlockSpec((s, tf), lo), pl.BlockSpec((s, tf), lambda j: (0, j + 2 * nf)),
                  pl.BlockSpec((s, tf), hi), pl.BlockSpec((s, tf), lambda j: (0, j + 3 * nf))],
        out_specs=[pl.BlockSpec((m, tf), lo), pl.BlockSpec((4, s, tf), lambda j: (0, 0, j))],
        out_shape=[jax.ShapeDtypeStruct((m, ff), BF16), jax.ShapeDtypeStruct((4, s, ff), F32)],
        scratch_shapes=[pltpu.VMEM((m, d), BF16),
                        pltpu.VMEM((2 * s + m, tf), F32), pltpu.VMEM((2 * s + m, tf), F32)],
        compiler_params=_params(1),
        name="ffn_up_slab",
    )(x, g.reshape(1, d), w_up, w_up, w_dw, w_dw, b_dw.reshape(1, f2), b_dw.reshape(1, f2),
      st2d, st2d, st2d, st2d)
    act, new = outs
    return act, new.reshape(2, 2, s, ff).transpose(2, 0, 1, 3).reshape(s, 2, f2)


def _front_pad(st, pad):
    return jnp.pad(st, ((0, 0), (pad - st.shape[1], 0), (0, 0)))


def _trunk(x, st_a, st_b, st_c, st_f, start, wts, *, slab):
    (norm_mix, norm_ffn, norm_final, a_w_pw1, a_w_dw, a_b_dw, a_ln_g, a_ln_b, a_w_pw2,
     b_w_grp, b_scale, c_lb, c_w_in, c_g_norm, c_w_o, f_w_up, f_w_dw, f_b_dw, f_w_down) = wts
    bsz, t, d = x.shape
    depth = norm_mix.shape[0]
    m = bsz * t
    if slab:
        to_rows = lambda a: a.transpose(1, 0, 2).reshape(m, a.shape[-1])
        from_rows = lambda a: a.reshape(t, bsz, a.shape[-1]).transpose(1, 0, 2)
    else:
        to_rows = lambda a: a.reshape(m, a.shape[-1])
        from_rows = lambda a: a.reshape(bsz, t, a.shape[-1])
    xr = to_rows(x)
    new_a, new_b, new_c, new_f = [], [], [], []
    for layer in range(depth):
        kind, j = layer % 3, layer // 3
        if kind == 0:
            v = _pro_mm(xr, a_w_pw1, j, mode="rms", p0=norm_mix[layer], glu=True, name="a_pw1")
            hist = a_w_dw.shape[1] - 1
            if slab:
                c_pre, nb = _conv_slab(v, st_a[j].reshape(bsz, -1), a_w_dw[j], a_b_dw[j], s=bsz)
                nb = nb.transpose(1, 0, 2)
            else:
                c_pre, nb = _conv_long(v, _front_pad(st_a[j], HIST_PAD), a_w_dw[j], a_b_dw[j], n_seq=bsz)
                nb = nb[:, HIST_PAD - hist:]
            new_a.append(nb)
            xr = _pro_mm(c_pre, a_w_pw2, j, mode="ln_silu", p0=a_ln_g[j], p1=a_ln_b[j], res=xr, name="a_pw2")
        elif kind == 1:
            h = _rms(xr, norm_mix[layer])
            hist = max(POOL_WINDOWS) - 1
            if slab:
                pooled, nb = _pool_slab(h, st_b[j].reshape(bsz, -1), s=bsz, start=start)
                nb = nb.transpose(1, 0, 2)
            else:
                pooled, nb = _pool_long(h, _front_pad(st_b[j], HIST_PAD), n_seq=bsz, start=start)
                nb = nb[:, HIST_PAD - hist:]
            new_b.append(nb)
            xr = _grp_mm(pooled, b_w_grp, j, b_scale[j], xr)
        else:
            h = _rms(xr, norm_mix[layer], BF16)
            parts = []
            for i, (post, arg) in enumerate((("silu", None), ("forget", c_lb), (None, None),
                                             ("silu_scale", c_g_norm[j]))):
                part = _pro_mm(h, c_w_in, j, mode="cast", cols=(i * d, d),
                               post=post, post_arg=arg, post_layer=layer, name="c_in")
                parts.append(from_rows(part).reshape(m, d) if slab else part)
            og, s_fin = _hgrn(*parts, st_c[j], n_seq=bsz)
            if slab:
                og = to_rows(og.reshape(bsz, t, d))
            new_c.append(s_fin)
            xr = _pro_mm(og, c_w_o, j, mode="cast", res=xr, name="c_out")
        if slab:
            act, nb = _ffn_up_slab(xr, norm_ffn[layer], f_w_up, layer, f_w_dw[layer], f_b_dw[layer],
                                   st_f[layer].reshape(bsz, -1), s=bsz)
        else:
            act, nb = _ffn_up_long(xr, norm_ffn[layer], f_w_up, layer, f_w_dw[layer], f_b_dw[layer],
                                   _front_pad(st_f[layer], FFN_HIST_PAD), n_seq=bsz)
        new_f.append(nb)
        xr = _pro_mm(act, f_w_down, layer, mode="cast", res=xr, name="f_down")
    y = from_rows(_rms(xr, norm_final))
    return y, jnp.stack(new_a), jnp.stack(new_b), jnp.stack(new_c), jnp.stack(new_f)


def kernel(x_prompt, x_sample, state_conv_a, state_pool, state_hgrn, state_ffn_conv, norm_mix, norm_ffn, norm_final, a_w_pw1, a_w_dw, a_b_dw, a_ln_g, a_ln_b, a_w_pw2, b_w_grp, b_scale, c_lb, c_w_in, c_g_norm, c_w_o, f_w_up, f_w_dw, f_b_dw, f_w_down):
    bf = lambda w: w.astype(BF16)
    wts = (norm_mix, norm_ffn, norm_final, bf(a_w_pw1), a_w_dw, a_b_dw, a_ln_g, a_ln_b, bf(a_w_pw2),
           bf(b_w_grp), b_scale, c_lb, bf(c_w_in), c_g_norm, bf(c_w_o), bf(f_w_up), f_w_dw, f_b_dw,
           bf(f_w_down))
    bp = x_prompt.shape[0]
    zeros = lambda st: jnp.zeros((st.shape[0], bp) + st.shape[2:], st.dtype)
    yp, pa, pb, pc, pf = _trunk(x_prompt, zeros(state_conv_a), zeros(state_pool), zeros(state_hgrn),
                                zeros(state_ffn_conv), 0, wts, slab=False)
    ys, sa, sb, sc, sf = _trunk(x_sample, state_conv_a, state_pool, state_hgrn, state_ffn_conv,
                                PAST_LEN, wts, slab=True)
    return (yp, ys, pa, sa, pb, sb, pc, sc, pf, sf)
```

```python
import functools

import numpy as np
import jax
import jax.numpy as jnp
from jax import lax
from jax.experimental import pallas as pl
from jax.experimental.pallas import tpu as pltpu

F32 = jnp.float32
BF16 = jnp.bfloat16

NORM_EPS = 1e-6
POOL_WINDOWS = (2, 4, 8, 16)
HGRN_DK = 128
LANES = 128
SUBLANES = 8
HIST_PAD = 32
FFN_HIST_PAD = 8
VMEM_LIMIT_BYTES = 56 * 1024 * 1024
PAST_LEN = 16384
ROWS_MM = 1024
COLS_MM = 1024
COLS_FFN = 512
MM_VMEM_BUDGET = 46 * 1024 * 1024
ROWS_CONV = 512
COLS_CONV = 512
ROWS_FFN = 1024


def _params(n_axes):
    return pltpu.CompilerParams(dimension_semantics=("arbitrary",) * n_axes,
                                vmem_limit_bytes=VMEM_LIMIT_BYTES)


def _tile(n, pref, mult):
    t = min(pref, n)
    t -= t % mult
    while t >= mult:
        if n % t == 0:
            return t
        t -= mult
    return n


def _sigmoid(x):
    return 1.0 / (1.0 + jnp.exp(-x))


def _silu(x):
    return x * _sigmoid(x)


def _rms_rows(x, g):
    ms = jnp.mean(x * x, axis=-1, keepdims=True)
    return x * lax.rsqrt(ms + NORM_EPS) * g


def _ln_silu_rows(x, g, b):
    mu = jnp.mean(x, axis=-1, keepdims=True)
    d = x - mu
    var = jnp.mean(d * d, axis=-1, keepdims=True)
    return _silu(d * lax.rsqrt(var + NORM_EPS) * g + b)


def _row_chunks(tm, fn):
    rc = _tile(tm, 128, SUBLANES)

    def body(i, carry):
        fn(pl.ds(pl.multiple_of(i * rc, rc), rc))
        return carry

    lax.fori_loop(0, tm // rc, body, 0)


def _rms_kernel(x_ref, g_ref, o_ref):
    tm = x_ref.shape[0]

    def chunk(rows):
        o_ref[rows, :] = _rms_rows(x_ref[rows, :], g_ref[...]).astype(o_ref.dtype)

    _row_chunks(tm, chunk)


def _rms(x, g, out_dtype=F32):
    m, d = x.shape
    tm = _tile(m, 512, 16)
    return pl.pallas_call(
        _rms_kernel,
        grid=(m // tm,),
        in_specs=[pl.BlockSpec((tm, d), lambda i: (i, 0)),
                  pl.BlockSpec((1, d), lambda i: (0, 0))],
        out_specs=pl.BlockSpec((tm, d), lambda i: (i, 0)),
        out_shape=jax.ShapeDtypeStruct((m, d), out_dtype),
        compiler_params=_params(1),
        name="rms",
    )(x, g.reshape(1, d))


def _forget_floor(clb, layer):
    e = jnp.exp(clb - jnp.max(clb, axis=0, keepdims=True))
    sm = e / jnp.sum(e, axis=0, keepdims=True)
    lb = jnp.zeros_like(sm[0:1])
    for i in range(1, layer + 1):
        lb = lb + sm[i:i + 1]
    return lb


def _pro_mm_kernel(*refs, mode, glu, has_res, post, post_layer):
    it = iter(refs)
    x_ref = next(it)
    p0_ref = next(it) if mode in ("rms", "ln_silu") else None
    p1_ref = next(it) if mode == "ln_silu" else None
    w_ref = next(it)
    wg_ref = next(it) if glu else None
    res_ref = next(it) if has_res else None
    e_ref = next(it) if post in ("forget", "silu_scale") else None
    o_ref = next(it)
    tm = x_ref.shape[0]

    a_ref = x_ref if x_ref.dtype == BF16 else next(it)
    rb = tm if (a_ref is x_ref and post is None) else _tile(tm, max(tm // 4, 16), 16)

    def prologue(rows0):
        step = _tile(rb, 32, 16)
        for r0 in range(rows0, rows0 + rb, step):
            xv = x_ref[r0:r0 + step, :]
            if mode == "rms":
                xv = _rms_rows(xv, p0_ref[...])
            elif mode == "ln_silu":
                xv = _ln_silu_rows(xv, p0_ref[...], p1_ref[...])
            a_ref[r0:r0 + step, :] = xv.astype(BF16)

    def product(rows0):
        rows = slice(rows0, rows0 + rb)
        a = a_ref[rows, :]
        acc = jnp.dot(a, w_ref[...], preferred_element_type=F32)
        if glu:
            acc = acc * _sigmoid(jnp.dot(a, wg_ref[...], preferred_element_type=F32))
        if has_res:
            acc = res_ref[rows, :] + acc
        if post == "silu":
            acc = _silu(acc)
        elif post == "silu_scale":
            acc = _silu(acc) * e_ref[...]
        elif post == "forget":
            lb = _forget_floor(e_ref[...], post_layer)
            acc = lb + (1.0 - lb) * _sigmoid(acc)
        o_ref[rows, :] = acc

    if a_ref is x_ref:
        for rows0 in range(0, tm, rb):
            product(rows0)
    else:
        @pl.when(pl.program_id(1) == 0)
        def _():
            for rows0 in range(0, tm, rb):
                prologue(rows0)
                product(rows0)

        @pl.when(pl.program_id(1) > 0)
        def _():
            for rows0 in range(0, tm, rb):
                product(rows0)


def _mm_vmem_bytes(tm, k, tn, x_bytes, glu, has_res):
    n_w = 2 if glu else 1
    blocks = 2 * (tm * k * x_bytes + n_w * k * tn * 2 + (2 if has_res else 1) * tm * tn * 4)
    return blocks + (tm * k * 2 if x_bytes == 4 else 0) + (n_w + 1) * tm * tn * 4


def _pro_mm(x, w, layer, *, mode, p0=None, p1=None, glu=False, res=None, cols=None,
            post=None, post_arg=None, post_layer=0, name):
    m, k = x.shape
    c0, n = cols if cols is not None else (0, w.shape[2] // (2 if glu else 1))
    tm = _tile(m, ROWS_MM, 16)
    tn = _tile(n, COLS_MM, LANES)
    while tn > LANES and _mm_vmem_bytes(tm, k, tn, x.dtype.itemsize, glu, res is not None) > MM_VMEM_BUDGET:
        tn = _tile(n, tn // 2, LANES)
    nn = n // tn
    args = [x]
    in_specs = [pl.BlockSpec((tm, k), lambda i, j: (i, 0))]
    for p in (p0, p1):
        if p is not None:
            args.append(p.reshape(1, k))
            in_specs.append(pl.BlockSpec((1, k), lambda i, j: (0, 0)))
    assert c0 % tn == 0
    j0 = c0 // tn
    args.append(w)
    in_specs.append(pl.BlockSpec((None, k, tn), lambda i, j: (layer, 0, j0 + j)))
    if glu:
        args.append(w)
        in_specs.append(pl.BlockSpec((None, k, tn), lambda i, j: (layer, 0, j0 + j + nn)))
    if res is not None:
        args.append(res)
        in_specs.append(pl.BlockSpec((tm, tn), lambda i, j: (i, j)))
    if post == "silu_scale":
        args.append(post_arg.reshape(1, n))
        in_specs.append(pl.BlockSpec((1, tn), lambda i, j: (0, j)))
    elif post == "forget":
        args.append(post_arg)
        in_specs.append(pl.BlockSpec((post_arg.shape[0], tn), lambda i, j: (0, j)))
    return pl.pallas_call(
        functools.partial(_pro_mm_kernel, mode=mode, glu=glu, has_res=res is not None,
                          post=post, post_layer=post_layer),
        grid=(m // tm, nn),
        in_specs=in_specs,
        out_specs=pl.BlockSpec((tm, tn), lambda i, j: (i, j)),
        out_shape=jax.ShapeDtypeStruct((m, n), F32),
        scratch_shapes=[] if x.dtype == BF16 else [pltpu.VMEM((tm, k), BF16)],
        compiler_params=_params(2),
        name=name,
    )(*args)


def _conv_long_kernel(v_ref, st_ref, w_ref, b_ref, c_ref, nb_ref, xp_ref, wk_ref, cs_ref, *, tps):
    tm, cb = v_ref.shape
    width = w_ref.shape[0]
    off = HIST_PAD - (width - 1)
    rc = _tile(tm, 128, SUBLANES)
    m = pl.program_id(1)
    start = (m % tps) == 0
    cols = [slice(j * LANES, (j + 1) * LANES) for j in range(cb // LANES)]

    @pl.when(start)
    def _():
        for j, cs in enumerate(cols):
            xp_ref[j, 0:HIST_PAD, :] = st_ref[0, :, cs]

    @pl.when(jnp.logical_not(start))
    def _():
        for j in range(len(cols)):
            xp_ref[j, 0:HIST_PAD, :] = xp_ref[j, tm:tm + HIST_PAD, :]

    for j, cs in enumerate(cols):
        xp_ref[j, HIST_PAD:HIST_PAD + tm, :] = v_ref[:, cs]
        wk_ref[j, 0:width, :] = w_ref[:, cs]
        wk_ref[j, width:width + 1, :] = b_ref[:, cs]

    def column(j, carry):
        for r in range(tm // rc):
            acc = jnp.broadcast_to(wk_ref[j, width:width + 1, :], (rc, LANES))
            for rho in range(min(SUBLANES, width)):
                taps = range(rho, width, SUBLANES)
                lo = r * rc + off + rho
                win = xp_ref[j, lo:lo + rc + SUBLANES * (len(taps) - 1), :]
                for i, k in enumerate(taps):
                    acc = acc + wk_ref[j, k:k + 1, :] * win[SUBLANES * i:SUBLANES * i + rc]
            cs_ref[j, r * rc:(r + 1) * rc, :] = acc
        return carry

    lax.fori_loop(0, len(cols), column, 0)
    for j, cs in enumerate(cols):
        c_ref[:, cs] = cs_ref[j]
        nb_ref[0, :, cs] = xp_ref[j, tm:tm + HIST_PAD, :]


def _conv_long(v, st, w_dw, b_dw, *, n_seq):
    m, c = v.shape
    width = w_dw.shape[0]
    t = m // n_seq
    tm = _tile(t, ROWS_CONV, SUBLANES)
    assert tm >= HIST_PAD
    tps = t // tm
    cb = _tile(c, COLS_CONV, LANES)
    ncl = cb // LANES
    return pl.pallas_call(
        functools.partial(_conv_long_kernel, tps=tps),
        grid=(c // cb, m // tm),
        in_specs=[pl.BlockSpec((tm, cb), lambda j, i: (i, j)),
                  pl.BlockSpec((1, HIST_PAD, cb), lambda j, i: (i // tps, 0, j)),
                  pl.BlockSpec((width, cb), lambda j, i: (0, j)),
                  pl.BlockSpec((1, cb), lambda j, i: (0, j))],
        out_specs=[pl.BlockSpec((tm, cb), lambda j, i: (i, j)),
                   pl.BlockSpec((1, HIST_PAD, cb), lambda j, i: (i // tps, 0, j))],
        out_shape=[jax.ShapeDtypeStruct((m, c), F32),
                   jax.ShapeDtypeStruct((n_seq, HIST_PAD, c), F32)],
        scratch_shapes=[pltpu.VMEM((ncl, HIST_PAD + tm, LANES), F32),
                        pltpu.VMEM((ncl, width + 1, LANES), F32),
                        pltpu.VMEM((ncl, tm, LANES), F32)],
        compiler_params=_params(2),
        name="conv_long",
    )(v, st, w_dw, b_dw.reshape(1, c))


def _conv_slab_kernel(v_ref, st_ref, w_ref, b_ref, c_ref, nb_ref, xs_ref, *, n_tok, s):
    width = w_ref.shape[0]
    hist = width - 1

    for i in range(hist):
        xs_ref[i] = st_ref[:, i, :]

    def slab(i):
        if i < hist:
            return xs_ref[i]
        return v_ref[(i - hist) * s:(i - hist + 1) * s, :]

    for t in range(n_tok):
        acc = jnp.broadcast_to(b_ref[...], (s, LANES))
        for k in range(width):
            acc = acc + w_ref[k:k + 1, :] * slab(t + k)
        c_ref[t * s:(t + 1) * s, :] = acc
    for j in range(hist):
        nb_ref[j] = slab(j + n_tok)


def _conv_slab(v, st, layer, w_dw, b_dw, *, s):
    m, c = v.shape
    width = w_dw.shape[0]
    hist = width - 1
    n_tok = m // s
    return pl.pallas_call(
        functools.partial(_conv_slab_kernel, n_tok=n_tok, s=s),
        grid=(c // LANES,),
        in_specs=[pl.BlockSpec((m, LANES), lambda j: (0, j)),
                  pl.BlockSpec((None, s, hist, LANES), lambda j: (layer, 0, 0, j)),
                  pl.BlockSpec((width, LANES), lambda j: (0, j)),
                  pl.BlockSpec((1, LANES), lambda j: (0, j))],
        out_specs=[pl.BlockSpec((m, LANES), lambda j: (0, j)),
                   pl.BlockSpec((hist, s, LANES), lambda j: (0, 0, j))],
        out_shape=[jax.ShapeDtypeStruct((m, c), F32),
                   jax.ShapeDtypeStruct((hist, s, c), F32)],
        scratch_shapes=[pltpu.VMEM((hist, s, LANES), F32)],
        compiler_params=_params(1),
        name="conv_slab",
    )(v, st, w_dw, b_dw.reshape(1, c))


def _select_by_group(gid, vals):
    out = vals[-1]
    for g in range(len(vals) - 2, -1, -1):
        out = jnp.where(gid == g, vals[g], out)
    return out


def _pool_long_kernel(h_ref, st_ref, p_ref, nb_ref, xp_ref, s1_ref, s2_ref, s3_ref, s4_ref,
                      *, tps, start, lanes_per_group):
    tm, cb = h_ref.shape
    n = HIST_PAD + tm
    m = pl.program_id(1)
    first = (m % tps) == 0
    cols = [slice(j * LANES, (j + 1) * LANES) for j in range(cb // LANES)]

    @pl.when(first)
    def _():
        for j, cs in enumerate(cols):
            xp_ref[j, 0:HIST_PAD, :] = st_ref[0, :, cs]

    @pl.when(jnp.logical_not(first))
    def _():
        for j in range(len(cols)):
            xp_ref[j, 0:HIST_PAD, :] = xp_ref[j, tm:tm + HIST_PAD, :]

    for j, cs in enumerate(cols):
        gid = (pl.program_id(0) * len(cols) + j) // lanes_per_group
        xp_ref[j, HIST_PAD:n, :] = h_ref[:, cs]
        s1_ref[8:n, :] = xp_ref[j, 8:n, :] + xp_ref[j, 7:n - 1, :]
        s2_ref[16:n, :] = s1_ref[16:n, :] + s1_ref[14:n - 2, :]
        s3_ref[24:n, :] = s2_ref[24:n, :] + s2_ref[20:n - 4, :]
        s4_ref[32:n, :] = s3_ref[32:n, :] + s3_ref[24:n - 8, :]
        sums = [r[HIST_PAD:n, :] for r in (s1_ref, s2_ref, s3_ref, s4_ref)]
        if start >= max(POOL_WINDOWS) - 1:
            means = [sm * (1.0 / w) for sm, w in zip(sums, POOL_WINDOWS)]
        else:
            pos = start + (m % tps) * tm + lax.broadcasted_iota(jnp.int32, (tm, LANES), 0)
            means = [sm / jnp.minimum(w, pos + 1).astype(F32) for sm, w in zip(sums, POOL_WINDOWS)]
        p_ref[:, cs] = (_select_by_group(gid, means) - h_ref[:, cs]).astype(p_ref.dtype)
        nb_ref[0, :, cs] = xp_ref[j, tm:tm + HIST_PAD, :]


def _pool_long(h, st, *, n_seq, start):
    m, d = h.shape
    t = m // n_seq
    tm = _tile(t, ROWS_CONV, 16)
    assert tm >= HIST_PAD
    tps = t // tm
    lanes_per_group = d // len(POOL_WINDOWS) // LANES
    cb = _tile(d, COLS_CONV, LANES)
    return pl.pallas_call(
        functools.partial(_pool_long_kernel, tps=tps, start=start, lanes_per_group=lanes_per_group),
        grid=(d // cb, m // tm),
        in_specs=[pl.BlockSpec((tm, cb), lambda j, i: (i, j)),
                  pl.BlockSpec((1, HIST_PAD, cb), lambda j, i: (i // tps, 0, j))],
        out_specs=[pl.BlockSpec((tm, cb), lambda j, i: (i, j)),
                   pl.BlockSpec((1, HIST_PAD, cb), lambda j, i: (i // tps, 0, j))],
        out_shape=[jax.ShapeDtypeStruct((m, d), BF16),
                   jax.ShapeDtypeStruct((n_seq, HIST_PAD, d), F32)],
        scratch_shapes=[pltpu.VMEM((cb // LANES, HIST_PAD + tm, LANES), F32)]
        + [pltpu.VMEM((HIST_PAD + tm, LANES), F32) for _ in range(4)],
        compiler_params=_params(2),
        name="pool_long",
    )(h, st)


def _pool_slab_kernel(h_ref, st_ref, p_ref, nb_ref, *, n_tok, s, start, lanes_per_group):
    hist = st_ref.shape[1]
    gid = pl.program_id(0) // lanes_per_group

    def slab(i):
        if i < hist:
            return st_ref[:, i, :]
        return h_ref[(i - hist) * s:(i - hist + 1) * s, :]

    n = hist + n_tok
    level = level0 = [slab(i) for i in range(n)]
    levels = []
    step = 1
    for _ in POOL_WINDOWS:
        level = [level[i] + level[i - step] if i >= 2 * step - 1 else None for i in range(n)]
        levels.append(level)
        step *= 2
    for t in range(n_tok):
        means = [lv[hist + t] * (1.0 / min(w, start + t + 1)) for lv, w in zip(levels, POOL_WINDOWS)]
        p_ref[t * s:(t + 1) * s, :] = (_select_by_group(gid, means) - level0[hist + t]).astype(p_ref.dtype)
    for j in range(hist):
        nb_ref[j] = level0[j + n_tok]


def _pool_slab(h, st, layer, *, s, start):
    m, d = h.shape
    hist = max(POOL_WINDOWS) - 1
    n_tok = m // s
    ncb = d // LANES
    lanes_per_group = d // len(POOL_WINDOWS) // LANES
    return pl.pallas_call(
        functools.partial(_pool_slab_kernel, n_tok=n_tok, s=s, start=start,
                          lanes_per_group=lanes_per_group),
        grid=(ncb,),
        in_specs=[pl.BlockSpec((m, LANES), lambda j: (0, j)),
                  pl.BlockSpec((None, s, hist, LANES), lambda j: (layer, 0, 0, j))],
        out_specs=[pl.BlockSpec((m, LANES), lambda j: (0, j)),
                   pl.BlockSpec((hist, s, LANES), lambda j: (0, 0, j))],
        out_shape=[jax.ShapeDtypeStruct((m, d), BF16),
                   jax.ShapeDtypeStruct((hist, s, d), F32)],
        compiler_params=_params(1),
        name="pool_slab",
    )(h, st)


def _grp_mm_kernel(p_ref, w_ref, sc_ref, res_ref, o_ref):
    y = jnp.dot(p_ref[...], w_ref[0], preferred_element_type=F32)
    o_ref[...] = res_ref[...] + y * sc_ref[...]


def _grp_mm(p, w_grp, layer, scale, res):
    m, d = p.shape
    _, ng, gc, _ = w_grp.shape
    tm = _tile(m, ROWS_MM, 16)
    return pl.pallas_call(
        _grp_mm_kernel,
        grid=(m // tm, ng),
        in_specs=[pl.BlockSpec((tm, gc), lambda i, g: (i, g)),
                  pl.BlockSpec((None, 1, gc, gc), lambda i, g: (layer, g, 0, 0)),
                  pl.BlockSpec((1, gc), lambda i, g: (0, g)),
                  pl.BlockSpec((tm, gc), lambda i, g: (i, g))],
        out_specs=pl.BlockSpec((tm, gc), lambda i, g: (i, g)),
        out_shape=jax.ShapeDtypeStruct((m, d), F32),
        compiler_params=_params(2),
        name="grp_mm",
    )(p, w_grp, scale.reshape(1, d), res)


def _cumsum_rows(x):
    n = x.shape[0]
    row = lax.broadcasted_iota(jnp.int32, x.shape, 0)
    d = 1
    while d < n:
        x = x + jnp.where(row >= d, pltpu.roll(x, d, axis=0), 0.0)
        d *= 2
    return x


def _level_table(c):
    t = np.arange(c)[:, None]
    s = np.arange(c)[None, :]
    x = np.maximum(t ^ s, 1)
    top = np.left_shift(1, np.floor(np.log2(x)).astype(np.int64))
    return np.where(s < t, top, np.where(s == t, 0, -1)).astype(np.int32)


def _level_ref(g_cum, length):
    c = g_cum.shape[0]
    if length >= SUBLANES:
        parts = [jnp.broadcast_to(g_cum[b + length - 1:b + length, :], (2 * length, HGRN_DK))
                 for b in range(0, c, 2 * length)]
        return parts[0] if len(parts) == 1 else jnp.concatenate(parts, axis=0)
    g3 = g_cum.reshape(c // SUBLANES, SUBLANES, HGRN_DK)
    sub = lax.broadcasted_iota(jnp.int32, g3.shape, 1)
    ref = None
    for b in range(SUBLANES - 2 * length, -1, -2 * length):
        cand = jnp.broadcast_to(g3[:, b + length - 1:b + length, :], g3.shape)
        ref = cand if ref is None else jnp.where(sub < b + 2 * length, cand, ref)
    return ref.reshape(c, HGRN_DK)


def _tile_pairs(q, kk, g_cum, inp):
    c = q.shape[0]
    t_idx = lax.broadcasted_iota(jnp.int32, (c, HGRN_DK), 0)
    ps = []
    for s in range(c):
        d = jnp.where(t_idx >= s, g_cum - g_cum[s:s + 1, :], -jnp.inf)
        ps.append(q * jnp.exp(d) * kk[s:s + 1, :])
    a_all = jnp.dot(jnp.concatenate(ps, axis=0), jnp.ones((HGRN_DK, LANES), F32),
                    preferred_element_type=F32)
    out = jnp.zeros((c, HGRN_DK), F32)
    for s in range(c):
        out = out + a_all[s * c:(s + 1) * c] * inp[s:s + 1, :]
    return out


def _hgrn_chunk(q, f, inp, s0, level_masks):
    c = q.shape[0]
    kk = 1.0 - f
    g_cum = _cumsum_rows(jnp.log(f))
    o = jnp.dot(q * jnp.exp(g_cum), s0, preferred_element_type=F32)
    if c == SUBLANES:
        o = o + _tile_pairs(q, kk, g_cum, inp)
    else:
        scores = jnp.zeros((c, c), F32)
        length = 0
        while length < c:
            if length == 0:
                qt, kt = q, kk
            else:
                e = jnp.exp(-jnp.abs(g_cum - _level_ref(g_cum, length)))
                qt, kt = q * e, kk * e
            sc = lax.dot_general(qt, kt, (((1,), (1,)), ((), ())), preferred_element_type=F32)
            scores = jnp.where(level_masks[length], sc, scores)
            length = max(1, 2 * length)
        o = o + jnp.dot(scores, inp, preferred_element_type=F32)
    g_end = g_cum[c - 1:c, :]
    kt = kk * jnp.exp(g_end - g_cum)
    upd = lax.dot_general(kt, inp, (((0,), (0,)), ((), ())), preferred_element_type=F32)
    decay = jnp.transpose(jnp.broadcast_to(jnp.exp(g_end), (HGRN_DK, HGRN_DK)))
    return o, decay * s0 + upd


def _hgrn_kernel(q_ref, f_ref, i_ref, g_ref, lv_ref, s0_ref, o_ref, sf_ref, *, hu, chunk):
    @pl.when(pl.program_id(2) == 0)
    def _():
        sf_ref[...] = s0_ref[...]

    sb = sf_ref.shape[0]
    tb = q_ref.shape[0] // sb
    lv = lv_ref[...]
    level_masks = {length: lv == length for length in [0] + [2 ** i for i in range(chunk.bit_length() - 1)]}
    for si in range(sb):
        for h in range(hu):
            hs = slice(h * HGRN_DK, (h + 1) * HGRN_DK)
            state = sf_ref[si, h]
            for cix in range(tb // chunk):
                rows = slice(si * tb + cix * chunk, si * tb + (cix + 1) * chunk)
                o, state = _hgrn_chunk(q_ref[rows, hs], f_ref[rows, hs], i_ref[rows, hs], state,
                                       level_masks)
                o = o * lax.rsqrt(jnp.mean(o * o, axis=-1, keepdims=True) + NORM_EPS)
                o_ref[rows, hs] = o * g_ref[rows, hs]
            sf_ref[si, h] = state


def _hgrn(q, f, inp, gate, s0, *, n_seq):
    m, d = q.shape
    nh = d // HGRN_DK
    t = m // n_seq
    chunk = _tile(t, 128, SUBLANES)
    tb = _tile(t, 2 * chunk, chunk)
    short = t == tb == chunk == SUBLANES
    hu = nh if short else min(4, nh)
    sb = _tile(n_seq, 4, 1) if short else 1
    nhg = nh // hu
    ntb = t // tb
    wblk = hu * HGRN_DK
    rblk = sb * tb
    rows = pl.BlockSpec((rblk, wblk), lambda b, hg, c: (b * ntb + c, hg))
    return pl.pallas_call(
        functools.partial(_hgrn_kernel, hu=hu, chunk=chunk),
        grid=(n_seq // sb, nhg, ntb),
        in_specs=[rows, rows, rows, rows,
                  pl.BlockSpec((chunk, chunk), lambda b, hg, c: (0, 0)),
                  pl.BlockSpec((sb, hu, HGRN_DK, HGRN_DK), lambda b, hg, c: (b, hg, 0, 0))],
        out_specs=[pl.BlockSpec((rblk, wblk), lambda b, hg, c: (b * ntb + c, hg)),
                   pl.BlockSpec((sb, hu, HGRN_DK, HGRN_DK), lambda b, hg, c: (b, hg, 0, 0))],
        out_shape=[jax.ShapeDtypeStruct((m, d), F32),
                   jax.ShapeDtypeStruct(s0.shape, F32)],
        compiler_params=_params(3),
        name="hgrn",
    )(q, f, inp, gate, jnp.asarray(_level_table(chunk)), s0)


def _ffn_norm(x_ref, g_ref, h_ref):
    def chunk(rows):
        h_ref[rows, :] = _rms_rows(x_ref[rows, :], g_ref[...]).astype(BF16)

    _row_chunks(x_ref.shape[0], chunk)


def _ffn_up_long_kernel(x_ref, g_ref, wa_ref, wb_ref, dwa_ref, dwb_ref, ba_ref, bb_ref,
                        sta_ref, stb_ref, act_ref, nba_ref, nbb_ref,
                        h_ref, xpa_ref, xpb_ref, cra_ref, crb_ref, twa_ref, twb_ref, *, tps):
    tm = x_ref.shape[0]
    tf = act_ref.shape[1]
    pad = FFN_HIST_PAD
    m = pl.program_id(0)
    f = pl.program_id(1)
    first = (m % tps) == 0
    cols = [slice(c * LANES, (c + 1) * LANES) for c in range(tf // LANES)]
    halves = ((xpa_ref, wa_ref, sta_ref, cra_ref, nba_ref), (xpb_ref, wb_ref, stb_ref, crb_ref, nbb_ref))
    rb = _tile(tm, max(tm // 4, 16), 16)
    rc = _tile(rb, 64, 16)
    n_blocks = tm // rb

    def fill_history(slot):
        @pl.when(first)
        def _():
            for xp_ref, _, st_ref, _, _ in halves:
                for c, cs in enumerate(cols):
                    xp_ref[slot, c, 0:pad, :] = st_ref[0, :, cs]

        @pl.when(jnp.logical_not(first))
        def _():
            for xp_ref, _, _, cr_ref, _ in halves:
                for c, cs in enumerate(cols):
                    xp_ref[slot, c, 0:pad, :] = cr_ref[f, :, cs]

    def project(slot, i):
        rows = slice(i * rb, (i + 1) * rb)
        for xp_ref, w_ref, _, _, _ in halves:
            u = jnp.dot(h_ref[rows, :], w_ref[...], preferred_element_type=F32)
            for c, cs in enumerate(cols):
                xp_ref[slot, c, pad + i * rb:pad + (i + 1) * rb, :] = u[:, cs]

    def save_tail(slot):
        for xp_ref, _, _, cr_ref, nb_ref in halves:
            for c, cs in enumerate(cols):
                tail = xp_ref[slot, c, tm:tm + pad, :]
                cr_ref[f, :, cs] = tail
                nb_ref[0, :, cs] = tail

    def load_taps():
        for tw_ref, dw_ref, b_ref in ((twa_ref, dwa_ref, ba_ref), (twb_ref, dwb_ref, bb_ref)):
            for k in range(3):
                tw_ref[k] = jnp.broadcast_to(dw_ref[k:k + 1, :], (pad, tf))
            tw_ref[3] = jnp.broadcast_to(b_ref[...], (pad, tf))

    def conv(xp_ref, tw_ref, slot, c, r0):
        def rows_back(k):
            return xp_ref[slot, c, r0 + pad - k:r0 + pad - k + rc, :].reshape(rc // pad, pad, LANES)

        tap = lambda k: tw_ref[k, :, cols[c]]
        return tap(0) * rows_back(2) + tap(1) * rows_back(1) + tap(2) * rows_back(0) + tap(3)

    def gate(slot, i):
        for r0 in range(i * rb, (i + 1) * rb, rc):
            for c, cs in enumerate(cols):
                ca = conv(xpa_ref, twa_ref, slot, c, r0)
                cb = conv(xpb_ref, twb_ref, slot, c, r0)
                act_ref[r0:r0 + rc, cs] = (_silu(ca) * cb).reshape(rc, LANES).astype(BF16)

    def norm(i):
        step = _tile(rb, 32, 16)
        for r0 in range(i * rb, (i + 1) * rb, step):
            h_ref[r0:r0 + step, :] = _rms_rows(x_ref[r0:r0 + step, :], g_ref[...]).astype(BF16)

    fill_history(0)
    load_taps()

    def step(with_norm):
        for i in range(n_blocks + 1):
            if with_norm and i < n_blocks:
                norm(i)
            if i < n_blocks:
                project(0, i)
            if i > 0:
                gate(0, i - 1)
        save_tail(0)

    @pl.when(f == 0)
    def _():
        step(True)

    @pl.when(f > 0)
    def _():
        step(False)


def _ffn_up_long(x, g, w_up, layer, w_dw, b_dw, st, *, n_seq):
    m, d = x.shape
    f2 = w_up.shape[2]
    ff = f2 // 2
    t = m // n_seq
    tm = _tile(t, ROWS_FFN, 16)
    tps = t // tm
    tf = _tile(ff, COLS_FFN, LANES)
    nf = ff // tf
    pad = FFN_HIST_PAD
    pj = gj = lambda j: j
    act, nba, nbb = pl.pallas_call(
        functools.partial(_ffn_up_long_kernel, tps=tps),
        grid=(m // tm, nf),
        in_specs=[pl.BlockSpec((tm, d), lambda i, j: (i, 0)),
                  pl.BlockSpec((1, d), lambda i, j: (0, 0)),
                  pl.BlockSpec((None, d, tf), lambda i, j: (layer, 0, pj(j))),
                  pl.BlockSpec((None, d, tf), lambda i, j: (layer, 0, pj(j) + nf)),
                  pl.BlockSpec((3, tf), lambda i, j: (0, gj(j))),
                  pl.BlockSpec((3, tf), lambda i, j: (0, gj(j) + nf)),
                  pl.BlockSpec((1, tf), lambda i, j: (0, gj(j))),
                  pl.BlockSpec((1, tf), lambda i, j: (0, gj(j) + nf)),
                  pl.BlockSpec((1, pad, tf), lambda i, j: (i // tps, 0, pj(j))),
                  pl.BlockSpec((1, pad, tf), lambda i, j: (i // tps, 0, pj(j) + nf))],
        out_specs=[pl.BlockSpec((tm, tf), lambda i, j: (i, gj(j))),
                   pl.BlockSpec((1, pad, tf), lambda i, j: (i, 0, pj(j))),
                   pl.BlockSpec((1, pad, tf), lambda i, j: (i, 0, pj(j)))],
        out_shape=[jax.ShapeDtypeStruct((m, ff), BF16),
                   jax.ShapeDtypeStruct((m // tm, pad, ff), F32),
                   jax.ShapeDtypeStruct((m // tm, pad, ff), F32)],
        scratch_shapes=[pltpu.VMEM((tm, d), BF16),
                        pltpu.VMEM((1, tf // LANES, pad + tm, LANES), F32),
                        pltpu.VMEM((1, tf // LANES, pad + tm, LANES), F32),
                        pltpu.VMEM((nf, pad, tf), F32), pltpu.VMEM((nf, pad, tf), F32),
                        pltpu.VMEM((4, pad, tf), F32), pltpu.VMEM((4, pad, tf), F32)],
        compiler_params=_params(2),
        name="ffn_up_long",
    )(x, g.reshape(1, d), w_up, w_up, w_dw, w_dw, b_dw.reshape(1, f2), b_dw.reshape(1, f2), st, st)
    last = slice(tps - 1, None, tps)
    return act, jnp.concatenate([nba[last, pad - 2:], nbb[last, pad - 2:]], axis=-1)


def _ffn_up_slab_kernel(x_ref, g_ref, wa_ref, wb_ref, dwa_ref, dwb_ref, ba_ref, bb_ref,
                        sa0_ref, sa1_ref, sb0_ref, sb1_ref, act_ref, new_ref,
                        h_ref, xpa_ref, xpb_ref, *, s):
    m = x_ref.shape[0]

    @pl.when(pl.program_id(0) == 0)
    def _():
        _ffn_norm(x_ref, g_ref, h_ref)

    def half(w_ref, dw_ref, b_ref, s0_ref, s1_ref, idx, xp_ref):
        u = jnp.dot(h_ref[...], w_ref[...], preferred_element_type=F32)
        xp_ref[0:s, :] = s0_ref[...]
        xp_ref[s:2 * s, :] = s1_ref[...]
        xp_ref[2 * s:2 * s + m, :] = u
        new_ref[idx] = xp_ref[m:m + s, :]
        new_ref[2 + idx] = xp_ref[m + s:m + 2 * s, :]
        return (dw_ref[0:1, :] * xp_ref[0:m, :] + dw_ref[1:2, :] * xp_ref[s:s + m, :]
                + dw_ref[2:3, :] * u + b_ref[...])

    ca = half(wa_ref, dwa_ref, ba_ref, sa0_ref, sa1_ref, 0, xpa_ref)
    cb = half(wb_ref, dwb_ref, bb_ref, sb0_ref, sb1_ref, 1, xpb_ref)
    act_ref[...] = (_silu(ca) * cb).astype(BF16)


def _ffn_up_slab(x, g, w_up, layer, w_dw, b_dw, st2d, *, s):
    m, d = x.shape
    f2 = w_up.shape[2]
    ff = f2 // 2
    tf = _tile(ff, COLS_FFN, LANES)
    nf = ff // tf
    lo = lambda j: (0, j)
    hi = lambda j: (0, j + nf)
    outs = pl.pallas_call(
        functools.partial(_ffn_up_slab_kernel, s=s),
        grid=(nf,),
        in_specs=[pl.BlockSpec((m, d), lambda j: (0, 0)),
                  pl.BlockSpec((1, d), lambda j: (0, 0)),
                  pl.BlockSpec((None, d, tf), lambda j: (layer, 0, j)),
                  pl.BlockSpec((None, d, tf), lambda j: (layer, 0, j + nf)),
                  pl.BlockSpec((3, tf), lo), pl.BlockSpec((3, tf), hi),
                  pl.BlockSpec((1, tf), lo), pl.BlockSpec((1, tf), hi),
                  pl.BlockSpec((s, tf), lo), pl.BlockSpec((s, tf), lambda j: (0, j + 2 * nf)),
                  pl.BlockSpec((s, tf), hi), pl.BlockSpec((s, tf), lambda j: (0, j + 3 * nf))],
        out_specs=[pl.BlockSpec((m, tf), lo), pl.BlockSpec((4, s, tf), lambda j: (0, 0, j))],
        out_shape=[jax.ShapeDtypeStruct((m, ff), BF16), jax.ShapeDtypeStruct((4, s, ff), F32)],
        scratch_shapes=[pltpu.VMEM((m, d), BF16),
                        pltpu.VMEM((2 * s + m, tf), F32), pltpu.VMEM((2 * s + m, tf), F32)],
        compiler_params=_params(1),
        name="ffn_up_slab",
    )(x, g.reshape(1, d), w_up, w_up, w_dw, w_dw, b_dw.reshape(1, f2), b_dw.reshape(1, f2),
      st2d, st2d, st2d, st2d)
    act, new = outs
    return act, new.reshape(2, 2, s, ff).transpose(2, 0, 1, 3).reshape(s, 2, f2)


def _front_pad(st, pad):
    return jnp.pad(st, ((0, 0), (pad - st.shape[1], 0), (0, 0)))


def _trunk(x, st_a, st_b, st_c, st_f, start, wts, *, slab):
    (norm_mix, norm_ffn, norm_final, a_w_pw1, a_w_dw, a_b_dw, a_ln_g, a_ln_b, a_w_pw2,
     b_w_grp, b_scale, c_lb, c_w_in, c_g_norm, c_w_o, f_w_up, f_w_dw, f_b_dw, f_w_down) = wts
    bsz, t, d = x.shape
    depth = norm_mix.shape[0]
    m = bsz * t
    if slab:
        to_rows = lambda a: a.transpose(1, 0, 2).reshape(m, a.shape[-1])
        from_rows = lambda a: a.reshape(t, bsz, a.shape[-1]).transpose(1, 0, 2)
    else:
        to_rows = lambda a: a.reshape(m, a.shape[-1])
        from_rows = lambda a: a.reshape(bsz, t, a.shape[-1])
    xr = to_rows(x)
    new_a, new_b, new_c, new_f = [], [], [], []
    for layer in range(depth):
        kind, j = layer % 3, layer // 3
        if kind == 0:
            v = _pro_mm(xr, a_w_pw1, j, mode="rms", p0=norm_mix[layer], glu=True, name="a_pw1")
            hist = a_w_dw.shape[1] - 1
            if slab:
                c_pre, nb = _conv_slab(v, st_a, j, a_w_dw[j], a_b_dw[j], s=bsz)
                nb = nb.transpose(1, 0, 2)
            else:
                c_pre, nb = _conv_long(v, _front_pad(st_a[j], HIST_PAD), a_w_dw[j], a_b_dw[j], n_seq=bsz)
                nb = nb[:, HIST_PAD - hist:]
            new_a.append(nb)
            xr = _pro_mm(c_pre, a_w_pw2, j, mode="ln_silu", p0=a_ln_g[j], p1=a_ln_b[j], res=xr, name="a_pw2")
        elif kind == 1:
            h = _rms(xr, norm_mix[layer])
            hist = max(POOL_WINDOWS) - 1
            if slab:
                pooled, nb = _pool_slab(h, st_b, j, s=bsz, start=start)
                nb = nb.transpose(1, 0, 2)
            else:
                pooled, nb = _pool_long(h, _front_pad(st_b[j], HIST_PAD), n_seq=bsz, start=start)
                nb = nb[:, HIST_PAD - hist:]
            new_b.append(nb)
            xr = _grp_mm(pooled, b_w_grp, j, b_scale[j], xr)
        else:
            h = _rms(xr, norm_mix[layer], BF16)
            parts = []
            for i, (post, arg) in enumerate((("silu", None), ("forget", c_lb), (None, None),
                                             ("silu_scale", c_g_norm[j]))):
                part = _pro_mm(h, c_w_in, j, mode="cast", cols=(i * d, d),
                               post=post, post_arg=arg, post_layer=layer, name="c_in")
                parts.append(from_rows(part).reshape(m, d) if slab else part)
            og, s_fin = _hgrn(*parts, st_c[j], n_seq=bsz)
            if slab:
                og = to_rows(og.reshape(bsz, t, d))
            new_c.append(s_fin)
            xr = _pro_mm(og, c_w_o, j, mode="cast", res=xr, name="c_out")
        if slab:
            act, nb = _ffn_up_slab(xr, norm_ffn[layer], f_w_up, layer, f_w_dw[layer], f_b_dw[layer],
                                   st_f[layer].reshape(bsz, -1), s=bsz)
        else:
            act, nb = _ffn_up_long(xr, norm_ffn[layer], f_w_up, layer, f_w_dw[layer], f_b_dw[layer],
                                   _front_pad(st_f[layer], FFN_HIST_PAD), n_seq=bsz)
        new_f.append(nb)
        xr = _pro_mm(act, f_w_down, layer, mode="cast", res=xr, name="f_down")
    y = from_rows(_rms(xr, norm_final))
    return y, jnp.stack(new_a), jnp.stack(new_b), jnp.stack(new_c), jnp.stack(new_f)


def kernel(x_prompt, x_sample, state_conv_a, state_pool, state_hgrn, state_ffn_conv, norm_mix, norm_ffn, norm_final, a_w_pw1, a_w_dw, a_b_dw, a_ln_g, a_ln_b, a_w_pw2, b_w_grp, b_scale, c_lb, c_w_in, c_g_norm, c_w_o, f_w_up, f_w_dw, f_b_dw, f_w_down):
    bf = lambda w: w.astype(BF16)
    wts = (norm_mix, norm_ffn, norm_final, bf(a_w_pw1), a_w_dw, a_b_dw, a_ln_g, a_ln_b, bf(a_w_pw2),
           bf(b_w_grp), b_scale, c_lb, bf(c_w_in), c_g_norm, bf(c_w_o), bf(f_w_up), f_w_dw, f_b_dw,
           bf(f_w_down))
    bp = x_prompt.shape[0]
    zeros = lambda st: jnp.zeros((st.shape[0], bp) + st.shape[2:], st.dtype)
    yp, pa, pb, pc, pf = _trunk(x_prompt, zeros(state_conv_a), zeros(state_pool), zeros(state_hgrn),
                                zeros(state_ffn_conv), 0, wts, slab=False)
    ys, sa, sb, sc, sf = _trunk(x_sample, state_conv_a, state_pool, state_hgrn, state_ffn_conv,
                                PAST_LEN, wts, slab=True)
    return (yp, ys, pa, sa, pb, sb, pc, sc, pf, sf)
```
